```python
import math
import jax, jax.numpy as jnp
from jax import lax
import numpy as np

D_MODEL = 2048
BATCH = 4
SEQ = 4096
DEPTH = 2
DEC_BATCH = 1
DEC_SEQ = 16384
PAST_LEN = 128

ATTN_HEADS = 16
HEAD_DIM = 64
ATTN_WIDTH = ATTN_HEADS * HEAD_DIM
CONV_WIDTH = D_MODEL - ATTN_WIDTH
CONV_KERNEL = 31
IN_WIDTH = 3 * ATTN_WIDTH + 2 * CONV_WIDTH
DILATED_BRANCHES = ((128, 1), (512, 4), (2048, 16))
QUERY_BLOCK = 64
ROPE_THETA = 10000.0
NEG_INF = -1e30
PEER_HEADS = 8
PEER_N_KEYS = 128
PEER_N_EXPERTS = PEER_N_KEYS * PEER_N_KEYS
PEER_TOPK = 16
PEER_QUERY_DIM = 256
PEER_HALF_DIM = PEER_QUERY_DIM // 2
PEER_CHUNK = 128
EPS = 1e-6

kernel_name = "hymba_longnet_conformer_peer_encoder"


def rmsnorm(x, g):
    xf = x.astype(jnp.float32)
    y = xf * lax.rsqrt(jnp.mean(xf * xf, axis=-1, keepdims=True) + EPS) * g.astype(jnp.float32)
    return y.astype(x.dtype)


def layernorm(x, g, b):
    xf = x.astype(jnp.float32)
    mu = jnp.mean(xf, axis=-1, keepdims=True)
    var = jnp.mean(jnp.square(xf - mu), axis=-1, keepdims=True)
    y = (xf - mu) * lax.rsqrt(var + EPS) * g.astype(jnp.float32) + b.astype(jnp.float32)
    return y.astype(x.dtype)


def apply_rope(t):
    S, E = t.shape[1], t.shape[3]
    inv_freq = ROPE_THETA ** (-jnp.arange(0, E, 2, dtype=jnp.float32) / E)
    ang = jnp.arange(S, dtype=jnp.float32)[:, None] * inv_freq[None, :]
    cos = jnp.cos(ang)[None, :, None, :]
    sin = jnp.sin(ang)[None, :, None, :]
    tf = t.astype(jnp.float32)
    t1, t2 = tf[..., : E // 2], tf[..., E // 2:]
    return jnp.concatenate([t1 * cos - t2 * sin, t2 * cos + t1 * sin], axis=-1).astype(t.dtype)


def dilated_branch(q, k, v, window, dilation):
    B, S, H, E = q.shape
    half = window // (2 * dilation)
    L = S // dilation
    nb = -(-L // QUERY_BLOCK)
    Lp = nb * QUERY_BLOCK
    W = QUERY_BLOCK + 2 * half
    n = B * dilation

    def to_sub(t):
        return t.reshape(B, L, dilation, H, E).transpose(0, 2, 1, 3, 4).reshape(n, L, H, E)

    qs = jnp.pad(to_sub(q), ((0, 0), (0, Lp - L), (0, 0), (0, 0))).reshape(n, nb, QUERY_BLOCK, H, E)
    kv_pad = ((0, 0), (half, Lp - L + half), (0, 0), (0, 0))
    kp = jnp.pad(to_sub(k), kv_pad)
    vp = jnp.pad(to_sub(v), kv_pad)
    gather_idx = jnp.arange(nb)[:, None] * QUERY_BLOCK + jnp.arange(W)[None, :]
    kw = jnp.take(kp, gather_idx, axis=1)
    vw = jnp.take(vp, gather_idx, axis=1)

    qi = jnp.arange(nb)[:, None] * QUERY_BLOCK + jnp.arange(QUERY_BLOCK)[None, :]
    kj = gather_idx - half
    rel = kj[:, None, :] - qi[:, :, None]
    valid = (jnp.abs(rel) <= half) & (kj[:, None, :] >= 0) & (kj[:, None, :] < L)

    s = jnp.einsum('nbqhe,nbkhe->nbhqk', qs.astype(jnp.float32), kw.astype(jnp.float32)) * (E ** -0.5)
    s = jnp.where(valid[None, :, None, :, :], s, NEG_INF)
    m = jnp.max(s, axis=-1)
    p = jnp.exp(s - m[..., None])
    l = jnp.sum(p, axis=-1)
    m = jnp.swapaxes(m, 2, 3)
    l = jnp.swapaxes(l, 2, 3)
    o = jnp.einsum('nbhqk,nbkhe->nbqhe', p, vw.astype(jnp.float32)) / l[..., None]

    def from_sub(t):
        rest = t.shape[3:]
        t = t.reshape((n, Lp) + rest)[:, :L]
        return t.reshape((B, dilation, L) + rest).swapaxes(1, 2).reshape((B, S) + rest)

    return from_sub(o), from_sub(m), from_sub(l)


def dilated_attention(q, k, v):
    outs, maxes, dens = [], [], []
    for window, dilation in DILATED_BRANCHES:
        o, m, l = dilated_branch(q, k, v, window, dilation)
        outs.append(o); maxes.append(m); dens.append(l)
    m_all = jnp.max(jnp.stack(maxes, 0), axis=0)
    wts = [l * jnp.exp(m - m_all) for m, l in zip(maxes, dens)]
    num = sum(w[..., None] * o for w, o in zip(wts, outs))
    return num / sum(wts)[..., None]


def mixer(h, w_in, q_norm_g, k_norm_g, conv_dw_w, conv_dw_b, conv_ln_g, conv_ln_b,
          attn_out_g, conv_out_g, w_out):
    B, S, _ = h.shape
    u = h @ w_in
    q = u[..., :ATTN_WIDTH].reshape(B, S, ATTN_HEADS, HEAD_DIM)
    k = u[..., ATTN_WIDTH:2 * ATTN_WIDTH].reshape(B, S, ATTN_HEADS, HEAD_DIM)
    v = u[..., 2 * ATTN_WIDTH:3 * ATTN_WIDTH].reshape(B, S, ATTN_HEADS, HEAD_DIM)
    a = u[..., 3 * ATTN_WIDTH:3 * ATTN_WIDTH + CONV_WIDTH]
    gate = u[..., 3 * ATTN_WIDTH + CONV_WIDTH:]

    q = apply_rope(rmsnorm(q, q_norm_g))
    k = apply_rope(rmsnorm(k, k_norm_g))
    attn = dilated_attention(q, k, v).astype(h.dtype).reshape(B, S, ATTN_WIDTH)

    c = a * jax.nn.sigmoid(gate)
    pad = CONV_KERNEL // 2
    c = lax.conv_general_dilated(c, conv_dw_w[:, None, :].astype(c.dtype), (1,), [(pad, pad)],
                                 dimension_numbers=('NWC', 'WIO', 'NWC'),
                                 feature_group_count=CONV_WIDTH) + conv_dw_b
    c = layernorm(c, conv_ln_g, conv_ln_b)
    c = jax.nn.silu(c)

    merged = jnp.concatenate([rmsnorm(attn, attn_out_g), rmsnorm(c, conv_out_g)], axis=-1)
    return merged @ w_out


def peer(h, peer_wq, peer_keys, peer_u, peer_v):
    B, S, D = h.shape
    T = B * S
    ht = h.reshape(T, D)
    q = (ht @ peer_wq).astype(jnp.float32).reshape(T, PEER_HEADS, 2, PEER_HALF_DIM)
    s = jnp.einsum('thcj,hcnj->thcn', q, peer_keys.astype(jnp.float32))
    v1, i1 = lax.top_k(s[:, :, 0], PEER_TOPK)
    v2, i2 = lax.top_k(s[:, :, 1], PEER_TOPK)
    cand = (v1[..., :, None] + v2[..., None, :]).reshape(T, PEER_HEADS, PEER_TOPK * PEER_TOPK)
    top_s, top_i = lax.top_k(cand, PEER_TOPK)
    e1 = jnp.take_along_axis(i1, top_i // PEER_TOPK, axis=-1)
    e2 = jnp.take_along_axis(i2, top_i % PEER_TOPK, axis=-1)
    experts = (e1 * PEER_N_KEYS + e2).reshape(T, PEER_HEADS * PEER_TOPK)
    gates = jax.nn.softmax(top_s, axis=-1).reshape(T, PEER_HEADS * PEER_TOPK)

    nc = T // PEER_CHUNK
    xs = ht.reshape(nc, PEER_CHUNK, D)
    es = experts.reshape(nc, PEER_CHUNK, -1)
    gs = gates.reshape(nc, PEER_CHUNK, -1)

    def block(args):
        xc, ec, gc = args
        u_sel = jnp.take(peer_u, ec, axis=0)
        act = jax.nn.gelu(jnp.einsum('tkd,td->tk', u_sel, xc).astype(jnp.float32), approximate=False)
        v_sel = jnp.take(peer_v, ec, axis=0)
        return jnp.einsum('tk,tkd->td', (gc * act).astype(xc.dtype), v_sel)

    out = lax.map(block, (xs, es, gs))
    return out.reshape(B, S, D)


def run_trunk(x, norm1_g, w_in, q_norm_g, k_norm_g, conv_dw_w, conv_dw_b, conv_ln_g, conv_ln_b,
              attn_out_g, conv_out_g, w_out, norm2_g, peer_wq, peer_keys, peer_u, peer_v):
    for l in range(DEPTH):
        x = x + mixer(rmsnorm(x, norm1_g[l]), w_in[l], q_norm_g[l], k_norm_g[l], conv_dw_w[l],
                      conv_dw_b[l], conv_ln_g[l], conv_ln_b[l], attn_out_g[l], conv_out_g[l], w_out[l])
        x = x + peer(rmsnorm(x, norm2_g[l]), peer_wq[l], peer_keys[l], peer_u[l], peer_v[l])
    return x


def setup_inputs(seed: int = 0) -> dict:
    key = jax.random.key(seed)
    ks = jax.random.split(key, 20)
    f32 = jnp.float32
    nrm = lambda k, shape, scale: jax.random.normal(k, shape, f32) * scale
    gain = lambda k, shape: 1.0 + 0.02 * jax.random.normal(k, shape, f32)
    return {
        "x_prompt": jax.random.normal(ks[0], (BATCH, SEQ, D_MODEL), f32),
        "x_sample": jax.random.normal(ks[1], (DEC_BATCH, DEC_SEQ, D_MODEL), f32),
        "norm1_g": gain(ks[2], (DEPTH, D_MODEL)),
        "w_in": nrm(ks[3], (DEPTH, D_MODEL, IN_WIDTH), D_MODEL ** -0.5),
        "q_norm_g": gain(ks[4], (DEPTH, HEAD_DIM)),
        "k_norm_g": gain(ks[5], (DEPTH, HEAD_DIM)),
        "conv_dw_w": nrm(ks[6], (DEPTH, CONV_KERNEL, CONV_WIDTH), CONV_KERNEL ** -0.5),
        "conv_dw_b": nrm(ks[7], (DEPTH, CONV_WIDTH), 0.02),
        "conv_ln_g": gain(ks[8], (DEPTH, CONV_WIDTH)),
        "conv_ln_b": nrm(ks[9], (DEPTH, CONV_WIDTH), 0.02),
        "attn_out_g": gain(ks[10], (DEPTH, ATTN_WIDTH)),
        "conv_out_g": gain(ks[11], (DEPTH, CONV_WIDTH)),
        "w_out": nrm(ks[12], (DEPTH, D_MODEL, D_MODEL), D_MODEL ** -0.5),
        "norm2_g": gain(ks[13], (DEPTH, D_MODEL)),
        "peer_wq": nrm(ks[14], (DEPTH, D_MODEL, PEER_HEADS * PEER_QUERY_DIM), D_MODEL ** -0.5),
        "peer_keys": nrm(ks[15], (DEPTH, PEER_HEADS, 2, PEER_N_KEYS, PEER_HALF_DIM), PEER_HALF_DIM ** -0.5),
        "peer_u": nrm(ks[16], (DEPTH, PEER_N_EXPERTS, D_MODEL), D_MODEL ** -0.5),
        "peer_v": nrm(ks[17], (DEPTH, PEER_N_EXPERTS, D_MODEL), PEER_HEADS ** -0.5),
    }


def reference(x_prompt, x_sample, norm1_g, w_in, q_norm_g, k_norm_g, conv_dw_w, conv_dw_b,
              conv_ln_g, conv_ln_b, attn_out_g, conv_out_g, w_out, norm2_g, peer_wq, peer_keys,
              peer_u, peer_v):
    y_prompt = run_trunk(x_prompt, norm1_g, w_in, q_norm_g, k_norm_g, conv_dw_w, conv_dw_b, conv_ln_g,
                         conv_ln_b, attn_out_g, conv_out_g, w_out, norm2_g, peer_wq, peer_keys, peer_u, peer_v)
    y_sample = run_trunk(x_sample, norm1_g, w_in, q_norm_g, k_norm_g, conv_dw_w, conv_dw_b, conv_ln_g,
                         conv_ln_b, attn_out_g, conv_out_g, w_out, norm2_g, peer_wq, peer_keys, peer_u, peer_v)
    return (y_prompt, y_sample)
```

```python
import functools
import math

import jax
import jax.numpy as jnp
from jax import lax
from jax.experimental import pallas as pl
from jax.experimental.pallas import tpu as pltpu

D_MODEL = 2048
ATTN_HEADS = 16
HEAD_DIM = 64
ATTN_WIDTH = ATTN_HEADS * HEAD_DIM
CONV_WIDTH = D_MODEL - ATTN_WIDTH
CONV_KERNEL = 31
IN_WIDTH = 3 * ATTN_WIDTH + 2 * CONV_WIDTH
DILATIONS = (1, 4, 16)
HALF_WINDOW = 64
ROPE_THETA = 10000.0
NEG_INF = -1e30
PEER_HEADS = 8
PEER_N_KEYS = 128
PEER_N_EXPERTS = PEER_N_KEYS * PEER_N_KEYS
PEER_TOPK = 16
PEER_HALF_DIM = 128
EPS = 1e-6

LANES = 128
VMEM_LIMIT = 56 * 1024 * 1024

_BF16 = jnp.bfloat16
_F32 = jnp.float32


def _params(semantics):
    return pltpu.CompilerParams(dimension_semantics=semantics, vmem_limit_bytes=VMEM_LIMIT)


def _in_proj_kernel(x_ref, g_ref, w_ref, o_ref, h_ref):
    @pl.when(pl.program_id(1) == 0)
    def _():
        x = x_ref[...]
        ms = jnp.mean(x * x, axis=-1, keepdims=True)
        h_ref[...] = (x * lax.rsqrt(ms + EPS) * g_ref[...]).astype(_BF16)

    o_ref[...] = jnp.dot(h_ref[...], w_ref[...], preferred_element_type=_F32).astype(o_ref.dtype)


def _in_proj(x, g, w_bf16, tm=1024, tn=1024):
    t, d = x.shape
    n = w_bf16.shape[1]
    return pl.pallas_call(
        _in_proj_kernel,
        grid=(t // tm, n // tn),
        in_specs=[
            pl.BlockSpec((tm, d), lambda i, j: (i, 0)),
            pl.BlockSpec((1, d), lambda i, j: (0, 0)),
            pl.BlockSpec((d, tn), lambda i, j: (0, j)),
        ],
        out_specs=pl.BlockSpec((tm, tn), lambda i, j: (i, j)),
        out_shape=jax.ShapeDtypeStruct((t, n), _BF16),
        scratch_shapes=[pltpu.VMEM((tm, d), _BF16)],
        compiler_params=_params(("parallel", "arbitrary")),
        name="in_proj",
    )(x, g.reshape(1, d), w_bf16)


def _rope_tables(seq):
    half = HEAD_DIM // 2
    inv_freq = ROPE_THETA ** (-jnp.arange(0, HEAD_DIM, 2, dtype=_F32) / HEAD_DIM)
    ang = jnp.arange(seq, dtype=_F32)[:, None] * inv_freq[None, :]
    cos, sin = jnp.cos(ang), jnp.sin(ang)
    zero = jnp.zeros_like(sin)
    cos_t = jnp.tile(cos, (1, LANES // half))
    sin_lo = jnp.tile(jnp.concatenate([-sin, zero], axis=1), (1, LANES // HEAD_DIM))
    sin_hi = jnp.tile(jnp.concatenate([zero, sin], axis=1), (1, LANES // HEAD_DIM))
    return cos_t, sin_lo, sin_hi


def _qk_prep_kernel(u_ref, g_ref, cos_ref, slo_ref, shi_ref, o_ref):
    is_q = pl.program_id(1) < ATTN_WIDTH // LANES
    x = u_ref[...].astype(_F32)
    lane = lax.broadcasted_iota(jnp.int32, x.shape, 1)
    first = lane < HEAD_DIM
    x2 = x * x
    ss0 = jnp.sum(jnp.where(first, x2, 0.0), axis=-1, keepdims=True)
    ss1 = jnp.sum(jnp.where(first, 0.0, x2), axis=-1, keepdims=True)
    ms = jnp.where(first, ss0, ss1) * (1.0 / HEAD_DIM)
    g = jnp.where(is_q, g_ref[0:1, :], g_ref[1:2, :])
    y = x * lax.rsqrt(ms + EPS) * g
    half = HEAD_DIM // 2
    y = y * cos_ref[...] + pltpu.roll(y, LANES - half, 1) * slo_ref[...] + pltpu.roll(y, half, 1) * shi_ref[...]
    y = y * jnp.where(is_q, HEAD_DIM ** -0.5, 1.0)
    o_ref[...] = y.astype(o_ref.dtype)


def _qk_prep(u, q_g, k_g, rope, seq, tr=1024):
    t = u.shape[0]
    nblk = seq // tr
    g2 = jnp.stack([jnp.tile(q_g, LANES // HEAD_DIM), jnp.tile(k_g, LANES // HEAD_DIM)])
    tab = pl.BlockSpec((tr, LANES), lambda i, j: (i % nblk, 0))
    return pl.pallas_call(
        _qk_prep_kernel,
        grid=(t // tr, 2 * ATTN_WIDTH // LANES),
        in_specs=[
            pl.BlockSpec((tr, LANES), lambda i, j: (i, j)),
            pl.BlockSpec((2, LANES), lambda i, j: (0, 0)),
            tab, tab, tab,
        ],
        out_specs=pl.BlockSpec((tr, LANES), lambda i, j: (i, j)),
        out_shape=jax.ShapeDtypeStruct((t, 2 * ATTN_WIDTH), _BF16),
        compiler_params=_params(("parallel", "parallel")),
        name="qk_prep",
    )(u, g2, *rope)


ATT_QB = 128
ATT_W = ATT_QB + 2 * HALF_WINDOW
ATT_HALO = HALF_WINDOW * max(DILATIONS)


def _attention_kernel(q_ref, k_ref, kl_ref, kr_ref, v_ref, vl_ref, vr_ref, o_ref,
                      q_s, k_s, v_s, acc_s, m_s, l_s, *, seq, chunk):
    nch = seq // chunk
    cpos = (pl.program_id(0) % nch) * chunk
    halo = ATT_HALO

    q_s[...] = q_ref[...].astype(_F32)
    k_s[0:halo, :] = kl_ref[...].astype(_F32)
    k_s[halo:halo + chunk, :] = k_ref[...].astype(_F32)
    k_s[halo + chunk:, :] = kr_ref[...].astype(_F32)
    v_s[0:halo, :] = vl_ref[...].astype(_F32)
    v_s[halo:halo + chunk, :] = v_ref[...].astype(_F32)
    v_s[halo + chunk:, :] = vr_ref[...].astype(_F32)

    lane = lax.broadcasted_iota(jnp.int32, (1, LANES), 1)
    head_mask = [(lane < HEAD_DIM).astype(_F32), (lane >= HEAD_DIM).astype(_F32)]
    first = lane < HEAD_DIM
    qi = lax.broadcasted_iota(jnp.int32, (ATT_QB, ATT_W), 0)
    kj = lax.broadcasted_iota(jnp.int32, (ATT_QB, ATT_W), 1)

    for bi, dil in enumerate(DILATIONS):
        nblk = chunk // dil // ATT_QB

        def body(n, carry, dil=dil, nblk=nblk, bi=bi):
            r = n // nblk
            l0 = (n % nblk) * ATT_QB
            q_start = l0 * dil + r
            k_start = halo + (l0 - HALF_WINDOW) * dil + r
            if dil == 1:
                q_idx = pl.ds(q_start, ATT_QB)
                k_idx = pl.ds(k_start, ATT_W)
            else:
                q_idx = pl.ds(q_start, ATT_QB, stride=dil)
                k_idx = pl.ds(k_start, ATT_W, stride=dil)
            q2 = q_s[q_idx, :]
            k2 = k_s[k_idx, :].astype(_BF16)
            v2 = v_s[k_idx, :]
            base = cpos + (l0 - HALF_WINDOW) * dil + r
            j_lo = (jnp.maximum(-base, 0) + dil - 1) // dil
            j_hi = (jnp.maximum(seq - base, 0) + dil - 1) // dil
            lo = jnp.maximum(qi, j_lo)
            hi = jnp.minimum(qi + 2 * HALF_WINDOW, j_hi - 1)
            valid = (kj >= lo) & (kj <= hi)
            o2 = jnp.zeros((ATT_QB, LANES), _F32)
            m2 = jnp.zeros((ATT_QB, LANES), _F32)
            l2 = jnp.zeros((ATT_QB, LANES), _F32)
            for hh in range(2):
                qh = (q2 * head_mask[hh]).astype(_BF16)
                s = lax.dot_general(qh, k2, (((1,), (1,)), ((), ())), preferred_element_type=_F32)
                s = jnp.where(valid, s, NEG_INF)
                m = jnp.max(s, axis=-1, keepdims=True)
                p = jnp.exp(s - m)
                l = jnp.sum(p, axis=-1, keepdims=True)
                vh = (v2 * head_mask[hh]).astype(_BF16)
                o2 = o2 + jnp.dot(p.astype(_BF16), vh, preferred_element_type=_F32)
                if hh == 0:
                    m2, l2 = jnp.broadcast_to(m, m2.shape), jnp.broadcast_to(l, l2.shape)
                else:
                    m2, l2 = jnp.where(first, m2, m), jnp.where(first, l2, l)
            if bi == 0:
                acc_s[q_idx, :] = o2
                m_s[q_idx, :] = m2
                l_s[q_idx, :] = l2
            else:
                m_old = m_s[q_idx, :]
                m_new = jnp.maximum(m_old, m2)
                a_old = jnp.exp(m_old - m_new)
                a_new = jnp.exp(m2 - m_new)
                acc_s[q_idx, :] = acc_s[q_idx, :] * a_old + o2 * a_new
                l_s[q_idx, :] = l_s[q_idx, :] * a_old + l2 * a_new
                m_s[q_idx, :] = m_new
            return carry

        lax.fori_loop(0, chunk // ATT_QB, body, 0)

    o_ref[...] = (acc_s[...] / l_s[...]).astype(o_ref.dtype)


def _attention(qk, u, seq, chunk=4096):
    t = u.shape[0]
    chunk = min(chunk, seq)
    halo = ATT_HALO
    per = chunk // halo
    last = t // halo - 1
    n_hp = ATTN_WIDTH // LANES
    main = lambda off: pl.BlockSpec((chunk, LANES), lambda g, h: (g, off + h))
    left = lambda off: pl.BlockSpec((halo, LANES), lambda g, h: (jnp.maximum(g * per - 1, 0), off + h))
    right = lambda off: pl.BlockSpec((halo, LANES), lambda g, h: (jnp.minimum((g + 1) * per, last), off + h))
    v_off = 2 * ATTN_WIDTH // LANES
    kern = functools.partial(_attention_kernel, seq=seq, chunk=chunk)
    return pl.pallas_call(
        kern,
        grid=(t // chunk, n_hp),
        in_specs=[main(0), main(n_hp), left(n_hp), right(n_hp), main(v_off), left(v_off), right(v_off)],
        out_specs=pl.BlockSpec((chunk, LANES), lambda g, h: (g, h)),
        out_shape=jax.ShapeDtypeStruct((t, ATTN_WIDTH), _BF16),
        scratch_shapes=[
            pltpu.VMEM((chunk, LANES), _F32),
            pltpu.VMEM((chunk + 2 * halo, LANES), _F32),
            pltpu.VMEM((chunk + 2 * halo, LANES), _F32),
            pltpu.VMEM((chunk, LANES), _F32),
            pltpu.VMEM((chunk, LANES), _F32),
            pltpu.VMEM((chunk, LANES), _F32),
        ],
        compiler_params=_params(("parallel", "parallel")),
        name="attention",
    )(qk, qk, qk, qk, u, u, u)


CONV_PAD = CONV_KERNEL // 2
CONV_HALO = 16
CONV_RB = 64
CONV_CB = 256


def _conv_kernel(a_ref, al_ref, ar_ref, g_ref, gl_ref, gr_ref, w_ref, b_ref, lng_ref, lnb_ref, o_ref,
                 c_s, y_s, *, seq, tr):
    nblk = seq // tr
    c = pl.program_id(0) % nblk

    def glu(a, g):
        return a.astype(_F32) * jax.nn.sigmoid(g.astype(_F32))

    c_s[0:CONV_HALO, :] = glu(al_ref[...], gl_ref[...]) * (c > 0).astype(_F32)
    c_s[CONV_HALO:CONV_HALO + tr, :] = glu(a_ref[...], g_ref[...])
    c_s[CONV_HALO + tr:, :] = glu(ar_ref[...], gr_ref[...]) * (c < nblk - 1).astype(_F32)

    shift = CONV_HALO - CONV_PAD
    for cb in range(CONV_WIDTH // CONV_CB):
        cols = slice(cb * CONV_CB, (cb + 1) * CONV_CB)
        w = w_ref[:, cols]
        bias = b_ref[:, cols]

        def body(rb, carry, cols=cols, w=w, bias=bias):
            i0 = pl.multiple_of(rb * CONV_RB, CONV_RB)
            win = c_s[pl.ds(i0, CONV_RB + 2 * CONV_HALO), cols]
            acc = jnp.zeros((CONV_RB, CONV_CB), _F32) + bias
            for tap in range(CONV_KERNEL):
                acc = acc + win[tap + shift:tap + shift + CONV_RB, :] * w[tap:tap + 1, :]
            y_s[pl.ds(i0, CONV_RB), cols] = acc
            return carry

        lax.fori_loop(0, tr // CONV_RB, body, 0)

    y = y_s[...]
    mu = jnp.mean(y, axis=-1, keepdims=True)
    yc = y - mu
    var = jnp.mean(yc * yc, axis=-1, keepdims=True)
    z = yc * lax.rsqrt(var + EPS) * lng_ref[...] + lnb_ref[...]
    o_ref[...] = (z * jax.nn.sigmoid(z)).astype(o_ref.dtype)


def _conv(u, w, b, ln_g, ln_b, seq, tr=512):
    t = u.shape[0]
    per = tr // CONV_HALO
    last = t // CONV_HALO - 1
    a_blk = 3 * ATTN_WIDTH // CONV_WIDTH
    g_blk = a_blk + 1
    main = lambda col: pl.BlockSpec((tr, CONV_WIDTH), lambda i: (i, col))
    left = lambda col: pl.BlockSpec((CONV_HALO, CONV_WIDTH), lambda i: (jnp.maximum(i * per - 1, 0), col))
    right = lambda col: pl.BlockSpec((CONV_HALO, CONV_WIDTH), lambda i: (jnp.minimum((i + 1) * per, last), col))
    vec = lambda rows: pl.BlockSpec((rows, CONV_WIDTH), lambda i: (0, 0))
    kern = functools.partial(_conv_kernel, seq=seq, tr=tr)
    return pl.pallas_call(
        kern,
        grid=(t // tr,),
        in_specs=[main(a_blk), left(a_blk), right(a_blk), main(g_blk), left(g_blk), right(g_blk),
                  vec(CONV_KERNEL), vec(1), vec(1), vec(1)],
        out_specs=pl.BlockSpec((tr, CONV_WIDTH), lambda i: (i, 0)),
        out_shape=jax.ShapeDtypeStruct((t, CONV_WIDTH), _BF16),
        scratch_shapes=[pltpu.VMEM((tr + 2 * CONV_HALO, CONV_WIDTH), _F32), pltpu.VMEM((tr, CONV_WIDTH), _F32)],
        compiler_params=_params(("parallel",)),
        name="conv",
    )(u, u, u, u, u, u, w, b.reshape(1, -1), ln_g.reshape(1, -1), ln_b.reshape(1, -1))


def _rms(x, g):
    return x * lax.rsqrt(jnp.mean(x * x, axis=-1, keepdims=True) + EPS) * g


def _out_proj_kernel(x_ref, at_ref, cv_ref, ga_ref, gc_ref, w_ref, g2_ref, x1_ref, h2_ref):
    a = _rms(at_ref[...].astype(_F32), ga_ref[...]).astype(_BF16)
    c = _rms(cv_ref[...].astype(_F32), gc_ref[...]).astype(_BF16)
    mix = jnp.dot(a, w_ref[0:ATTN_WIDTH, :], preferred_element_type=_F32)
    mix = mix + jnp.dot(c, w_ref[ATTN_WIDTH:, :], preferred_element_type=_F32)
    x1 = x_ref[...] + mix
    x1_ref[...] = x1
    h2_ref[...] = _rms(x1, g2_ref[...]).astype(h2_ref.dtype)


def _out_proj(x, attn, conv, attn_g, conv_g, w_bf16, norm2_g, tm=512):
    t, d = x.shape
    row = lambda w: pl.BlockSpec((tm, w), lambda i: (i, 0))
    vec = lambda w: pl.BlockSpec((1, w), lambda i: (0, 0))
    return pl.pallas_call(
        _out_proj_kernel,
        grid=(t // tm,),
        in_specs=[row(d), row(ATTN_WIDTH), row(CONV_WIDTH), vec(ATTN_WIDTH), vec(CONV_WIDTH),
                  pl.BlockSpec((d, d), lambda i: (0, 0)), vec(d)],
        out_specs=[row(d), row(d)],
        out_shape=[jax.ShapeDtypeStruct((t, d), _F32), jax.ShapeDtypeStruct((t, d), _BF16)],
        compiler_params=_params(("parallel",)),
        name="out_proj",
    )(x, attn, conv, attn_g.reshape(1, -1), conv_g.reshape(1, -1), w_bf16, norm2_g.reshape(1, -1))


def _rank_top(work, order, rounds):
    big = float(2 ** 30)
    rank = jnp.full(work.shape, float(rounds), _F32)
    values = []
    for a in range(rounds):
        m = jnp.max(work, axis=0, keepdims=True)
        first = jnp.min(jnp.where(work == m, order, big), axis=0, keepdims=True)
        sel = order == first
        rank = jnp.where(sel, float(a), rank)
        work = jnp.where(sel, -jnp.inf, work)
        values.append(m)
    return rank, values


def _stack_rows(rows):
    sub = lax.broadcasted_iota(jnp.int32, (len(rows), rows[0].shape[1]), 0)
    out = jnp.broadcast_to(rows[-1], sub.shape)
    for i in range(len(rows) - 2, -1, -1):
        out = jnp.where(sub == i, rows[i], out)
    return out


ROUTE_NB = 8


def _route_tile(s1, s2):
    k = PEER_TOPK
    key_order = lax.broadcasted_iota(jnp.int32, s1.shape, 0).astype(_F32)
    rank1, v1 = _rank_top(s1, key_order, k)
    rank2, v2 = _rank_top(s2, key_order, k)
    v2_lo, v2_hi = _stack_rows(v2[:8]), _stack_rows(v2[8:])
    sub8 = lax.broadcasted_iota(jnp.int32, v2_lo.shape, 0).astype(_F32)
    sums = [v1[0] + v2_lo, v1[0] + v2_hi] + [v1[a] + v2_lo for a in range(1, k)]
    order = [sub8, sub8 + 8.0] + [sub8 + float(k * a) for a in range(1, k)]
    cand = jnp.concatenate(sums, axis=0)
    cand_rank, _ = _rank_top(cand, jnp.concatenate(order, axis=0), k)
    chosen = (cand_rank < float(k)).astype(_F32)
    top = v1[0] + v2[0]
    z = jnp.sum(chosen * jnp.exp(cand - top), axis=0, keepdims=True)
    n = [jnp.sum(chosen[0:2 * 8], axis=0, keepdims=True)]
    for a in range(1, k):
        n.append(jnp.sum(chosen[(a + 1) * 8:(a + 2) * 8], axis=0, keepdims=True))
    n1 = jnp.zeros_like(s1)
    for a in range(k):
        n1 = jnp.where(rank1 == float(a), n[a], n1)
    c1 = jnp.exp(s1 - v1[0]) / z
    p2 = jnp.exp(s2 - v2[0])
    return rank2, p2, n1, c1


def _route_kernel(h_ref, wq_ref, keys_ref, r2_ref, p2_ref, n1_ref, c1_ref, q_s, *, tm):
    qt = lax.dot_general(wq_ref[...], h_ref[...], (((1,), (1,)), ((), ())), preferred_element_type=_F32)
    groups = tm // LANES
    for g in range(groups):
        q_s[g] = qt[:, g * LANES:(g + 1) * LANES].astype(_BF16)

    def body(it, carry):
        h = it // groups
        g = it % groups
        row = pl.multiple_of(h * 2 * PEER_HALF_DIM, 2 * PEER_HALF_DIM)
        q1 = q_s[g, pl.ds(row, PEER_HALF_DIM), :]
        q2 = q_s[g, pl.ds(row + PEER_HALF_DIM, PEER_HALF_DIM), :]
        s1 = jnp.dot(keys_ref[2 * h], q1, preferred_element_type=_F32)
        s2 = jnp.dot(keys_ref[2 * h + 1], q2, preferred_element_type=_F32)
        rank2, p2, n1, c1 = _route_tile(s1, s2)
        r2_ref[h, g] = rank2
        p2_ref[h, g] = p2
        n1_ref[h, g] = n1
        c1_ref[h, g] = c1
        return carry

    lax.fori_loop(0, PEER_HEADS * groups, body, 0)


def _route(h2, wq_t_bf16, keys_bf16, tm=256):
    t, d = h2.shape
    groups = tm // LANES
    out = jax.ShapeDtypeStruct((PEER_HEADS, t // LANES, PEER_N_KEYS, LANES), _F32)
    ospec = pl.BlockSpec((PEER_HEADS, groups, PEER_N_KEYS, LANES), lambda i: (0, i, 0, 0))
    kern = functools.partial(_route_kernel, tm=tm)
    return pl.pallas_call(
        kern,
        grid=(t // tm,),
        in_specs=[
            pl.BlockSpec((tm, d), lambda i: (i, 0)),
            pl.BlockSpec(wq_t_bf16.shape, lambda i: (0, 0)),
            pl.BlockSpec(keys_bf16.shape, lambda i: (0, 0, 0)),
        ],
        out_specs=[ospec] * 4,
        out_shape=[out] * 4,
        scratch_shapes=[pltpu.VMEM((groups, wq_t_bf16.shape[0], LANES), _BF16)],
        compiler_params=_params(("parallel",)),
        name="route",
    )(h2, wq_t_bf16, keys_bf16)


def _gelu(x):
    return 0.5 * x * (1.0 + lax.erf(x * (1.0 / math.sqrt(2.0))))


def _experts_kernel(h_ref, u_ref, vt_ref, r2_ref, p2_ref, n1_ref, c1_ref, x1_ref, o_ref, acc_s, a_s, *, te):
    j = pl.program_id(1)

    @pl.when(j == 0)
    def _():
        acc_s[...] = jnp.zeros_like(acc_s)

    act = lax.dot_general(u_ref[...], h_ref[...], (((1,), (1,)), ((), ())), preferred_element_type=_F32)
    rows = te // PEER_N_KEYS
    groups = act.shape[1] // LANES
    for er in range(rows):
        e1 = j * rows + er
        sl = slice(er * PEER_N_KEYS, (er + 1) * PEER_N_KEYS)
        for g in range(groups):
            gl = slice(g * LANES, (g + 1) * LANES)
            gate = jnp.zeros((PEER_N_KEYS, LANES), _F32)
            for h in range(PEER_HEADS):
                n1 = n1_ref[h, g, pl.ds(e1, 1), :]
                c1 = c1_ref[h, g, pl.ds(e1, 1), :]
                gate = gate + jnp.where(r2_ref[h, g] < n1, p2_ref[h, g] * c1, 0.0)
            a_s[sl, gl] = (_gelu(act[sl, gl]) * gate).astype(_BF16)
    acc_s[...] += jnp.dot(vt_ref[...], a_s[...], preferred_element_type=_F32)

    @pl.when(j == pl.num_programs(1) - 1)
    def _():
        o_ref[...] = x1_ref[...] + acc_s[...].T


def _experts(h2, x1, u_bf16, vt_bf16, route, tm=512, te=512):
    t, d = h2.shape
    ne = u_bf16.shape[0]
    once = pl.Buffered(1)
    rspec = pl.BlockSpec((PEER_HEADS, tm // LANES, PEER_N_KEYS, LANES), lambda i, j: (0, i, 0, 0),
                         pipeline_mode=once)
    kern = functools.partial(_experts_kernel, te=te)
    return pl.pallas_call(
        kern,
        grid=(t // tm, ne // te),
        in_specs=[
            pl.BlockSpec((tm, d), lambda i, j: (i, 0), pipeline_mode=once),
            pl.BlockSpec((te, d), lambda i, j: (j, 0)),
            pl.BlockSpec((d, te), lambda i, j: (0, j)),
            rspec, rspec, rspec, rspec,
            pl.BlockSpec((tm, d), lambda i, j: (i, 0), pipeline_mode=once),
        ],
        out_specs=pl.BlockSpec((tm, d), lambda i, j: (i, 0)),
        out_shape=jax.ShapeDtypeStruct((t, d), _F32),
        scratch_shapes=[pltpu.VMEM((d, tm), _F32), pltpu.VMEM((te, tm), _BF16)],
        compiler_params=_params(("parallel", "arbitrary")),
        name="experts",
    )(h2, u_bf16, vt_bf16, *route, x1)


def _trunk(x, seq, layers, rope):
    b, s, d = x.shape
    x = x.reshape(b * s, d)
    for p in layers:
        u = _in_proj(x, p["norm1_g"], p["w_in"])
        qk = _qk_prep(u, p["q_norm_g"], p["k_norm_g"], rope, seq)
        attn = _attention(qk, u, seq)
        conv = _conv(u, p["conv_dw_w"], p["conv_dw_b"], p["conv_ln_g"], p["conv_ln_b"], seq)
        x1, h2 = _out_proj(x, attn, conv, p["attn_out_g"], p["conv_out_g"], p["w_out"], p["norm2_g"])
        route = _route(h2, p["peer_wq_t"], p["peer_keys"])
        x = _experts(h2, x1, p["peer_u"], p["peer_v_t"], route)
    return x.reshape(b, s, d)


def kernel(x_prompt, x_sample, norm1_g, w_in, q_norm_g, k_norm_g, conv_dw_w, conv_dw_b, conv_ln_g, conv_ln_b,
           attn_out_g, conv_out_g, w_out, norm2_g, peer_wq, peer_keys, peer_u, peer_v):
    depth = w_in.shape[0]
    layers = []
    for l in range(depth):
        layers.append(dict(
            norm1_g=norm1_g[l], w_in=w_in[l].astype(_BF16), q_norm_g=q_norm_g[l], k_norm_g=k_norm_g[l],
            conv_dw_w=conv_dw_w[l], conv_dw_b=conv_dw_b[l], conv_ln_g=conv_ln_g[l], conv_ln_b=conv_ln_b[l],
            attn_out_g=attn_out_g[l], conv_out_g=conv_out_g[l], w_out=w_out[l].astype(_BF16), norm2_g=norm2_g[l],
            peer_wq_t=peer_wq[l].T.astype(_BF16),
            peer_keys=peer_keys[l].reshape(2 * PEER_HEADS, PEER_N_KEYS, PEER_HALF_DIM).astype(_BF16),
            peer_u=peer_u[l].astype(_BF16), peer_v_t=peer_v[l].T.astype(_BF16),
        ))
    outs = []
    for x in (x_prompt, x_sample):
        seq = x.shape[1]
        outs.append(_trunk(x, seq, layers, _rope_tables(seq)))
    return tuple(outs)
```

```python
import functools
import math

import jax
import jax.numpy as jnp
from jax import lax
from jax.experimental import pallas as pl
from jax.experimental.pallas import tpu as pltpu

D_MODEL = 2048
ATTN_HEADS = 16
HEAD_DIM = 64
ATTN_WIDTH = ATTN_HEADS * HEAD_DIM
CONV_WIDTH = D_MODEL - ATTN_WIDTH
CONV_KERNEL = 31
IN_WIDTH = 3 * ATTN_WIDTH + 2 * CONV_WIDTH
DILATIONS = (1, 4, 16)
HALF_WINDOW = 64
ROPE_THETA = 10000.0
NEG_INF = -1e30
PEER_HEADS = 8
PEER_N_KEYS = 128
PEER_N_EXPERTS = PEER_N_KEYS * PEER_N_KEYS
PEER_TOPK = 16
PEER_HALF_DIM = 128
EPS = 1e-6

LANES = 128
VMEM_LIMIT = 56 * 1024 * 1024

_BF16 = jnp.bfloat16
_F32 = jnp.float32


def _params(semantics):
    return pltpu.CompilerParams(dimension_semantics=semantics, vmem_limit_bytes=VMEM_LIMIT)


def _in_proj_kernel(x_ref, g_ref, w_ref, o_ref, h_ref):
    @pl.when(pl.program_id(1) == 0)
    def _():
        x = x_ref[...]
        ms = jnp.mean(x * x, axis=-1, keepdims=True)
        h_ref[...] = (x * lax.rsqrt(ms + EPS) * g_ref[...]).astype(_BF16)

    o_ref[...] = jnp.dot(h_ref[...], w_ref[...], preferred_element_type=_F32).astype(o_ref.dtype)


def _in_proj(x, g, w_bf16, tm=1024, tn=1024):
    t, d = x.shape
    n = w_bf16.shape[1]
    return pl.pallas_call(
        _in_proj_kernel,
        grid=(t // tm, n // tn),
        in_specs=[
            pl.BlockSpec((tm, d), lambda i, j: (i, 0)),
            pl.BlockSpec((1, d), lambda i, j: (0, 0)),
            pl.BlockSpec((d, tn), lambda i, j: (0, j)),
        ],
        out_specs=pl.BlockSpec((tm, tn), lambda i, j: (i, j)),
        out_shape=jax.ShapeDtypeStruct((t, n), _BF16),
        scratch_shapes=[pltpu.VMEM((tm, d), _BF16)],
        compiler_params=_params(("parallel", "arbitrary")),
        name="in_proj",
    )(x, g.reshape(1, d), w_bf16)


def _rope_tables(seq):
    half = HEAD_DIM // 2
    inv_freq = ROPE_THETA ** (-jnp.arange(0, HEAD_DIM, 2, dtype=_F32) / HEAD_DIM)
    ang = jnp.arange(seq, dtype=_F32)[:, None] * inv_freq[None, :]
    cos, sin = jnp.cos(ang), jnp.sin(ang)
    zero = jnp.zeros_like(sin)
    cos_t = jnp.tile(cos, (1, LANES // half))
    sin_lo = jnp.tile(jnp.concatenate([-sin, zero], axis=1), (1, LANES // HEAD_DIM))
    sin_hi = jnp.tile(jnp.concatenate([zero, sin], axis=1), (1, LANES // HEAD_DIM))
    return cos_t, sin_lo, sin_hi


def _qk_prep_kernel(u_ref, g_ref, cos_ref, slo_ref, shi_ref, o_ref):
    is_q = pl.program_id(1) < ATTN_WIDTH // LANES
    x = u_ref[...].astype(_F32)
    lane = lax.broadcasted_iota(jnp.int32, x.shape, 1)
    first = lane < HEAD_DIM
    x2 = x * x
    ss0 = jnp.sum(jnp.where(first, x2, 0.0), axis=-1, keepdims=True)
    ss1 = jnp.sum(jnp.where(first, 0.0, x2), axis=-1, keepdims=True)
    ms = jnp.where(first, ss0, ss1) * (1.0 / HEAD_DIM)
    g = jnp.where(is_q, g_ref[0:1, :], g_ref[1:2, :])
    y = x * lax.rsqrt(ms + EPS) * g
    half = HEAD_DIM // 2
    y = y * cos_ref[...] + pltpu.roll(y, LANES - half, 1) * slo_ref[...] + pltpu.roll(y, half, 1) * shi_ref[...]
    y = y * jnp.where(is_q, HEAD_DIM ** -0.5, 1.0)
    o_ref[...] = y.astype(o_ref.dtype)


def _qk_prep(u, q_g, k_g, rope, seq, tr=1024):
    t = u.shape[0]
    nblk = seq // tr
    g2 = jnp.stack([jnp.tile(q_g, LANES // HEAD_DIM), jnp.tile(k_g, LANES // HEAD_DIM)])
    tab = pl.BlockSpec((tr, LANES), lambda i, j: (i % nblk, 0))
    return pl.pallas_call(
        _qk_prep_kernel,
        grid=(t // tr, 2 * ATTN_WIDTH // LANES),
        in_specs=[
            pl.BlockSpec((tr, LANES), lambda i, j: (i, j)),
            pl.BlockSpec((2, LANES), lambda i, j: (0, 0)),
            tab, tab, tab,
        ],
        out_specs=pl.BlockSpec((tr, LANES), lambda i, j: (i, j)),
        out_shape=jax.ShapeDtypeStruct((t, 2 * ATTN_WIDTH), _BF16),
        compiler_params=_params(("parallel", "parallel")),
        name="qk_prep",
    )(u, g2, *rope)


ATT_QB = 128
ATT_W = ATT_QB + 2 * HALF_WINDOW
ATT_HALO = HALF_WINDOW * max(DILATIONS)
ATT_UNROLL = 2


def _attention_kernel(q_ref, k_ref, kl_ref, kr_ref, v_ref, vl_ref, vr_ref, o_ref,
                      q_s, k_s, v_s, acc_s, m_s, l_s, *, seq, chunk):
    nch = seq // chunk
    cpos = (pl.program_id(0) % nch) * chunk
    halo = ATT_HALO

    q_s[...] = q_ref[...].astype(_F32)
    k_s[0:halo, :] = kl_ref[...].astype(_F32)
    k_s[halo:halo + chunk, :] = k_ref[...].astype(_F32)
    k_s[halo + chunk:, :] = kr_ref[...].astype(_F32)
    v_s[0:halo, :] = vl_ref[...].astype(_F32)
    v_s[halo:halo + chunk, :] = v_ref[...].astype(_F32)
    v_s[halo + chunk:, :] = vr_ref[...].astype(_F32)

    lane = lax.broadcasted_iota(jnp.int32, (1, LANES), 1)
    first = lane < HEAD_DIM
    row2 = lax.broadcasted_iota(jnp.int32, (2 * ATT_QB, LANES), 0)
    lane2 = lax.broadcasted_iota(jnp.int32, (2 * ATT_QB, LANES), 1)
    own = ((row2 < ATT_QB) == (lane2 < HEAD_DIM)).astype(_F32)
    qi = lax.broadcasted_iota(jnp.int32, (2 * ATT_QB, ATT_W), 0) % ATT_QB
    kj = lax.broadcasted_iota(jnp.int32, (2 * ATT_QB, ATT_W), 1)
    nt = (((1,), (1,)), ((), ()))

    for bi, dil in enumerate(DILATIONS):
        nblk = chunk // dil // ATT_QB

        def scores(n, dil=dil, nblk=nblk):
            r = n // nblk
            l0 = (n % nblk) * ATT_QB
            q_start = l0 * dil + r
            k_start = halo + (l0 - HALF_WINDOW) * dil + r
            if dil == 1:
                q_idx = pl.ds(q_start, ATT_QB)
                k_idx = pl.ds(k_start, ATT_W)
            else:
                q_idx = pl.ds(q_start, ATT_QB, stride=dil)
                k_idx = pl.ds(k_start, ATT_W, stride=dil)
            q2 = q_s[q_idx, :]
            qs = (jnp.concatenate([q2, q2], axis=0) * own).astype(_BF16)
            s = lax.dot_general(qs, k_s[k_idx, :].astype(_BF16), nt, preferred_element_type=_F32)
            base = cpos + (l0 - HALF_WINDOW) * dil + r
            j_lo = (jnp.maximum(-base, 0) + dil - 1) // dil
            j_hi = (jnp.maximum(seq - base, 0) + dil - 1) // dil
            lo = jnp.maximum(qi, j_lo)
            hi = jnp.minimum(qi + 2 * HALF_WINDOW, j_hi - 1)
            return q_idx, k_idx, jnp.where((kj >= lo) & (kj <= hi), s, NEG_INF)

        def softmax(s):
            m = jnp.max(s, axis=-1, keepdims=True)
            p = jnp.exp(s - m)
            return m, jnp.sum(p, axis=-1, keepdims=True), p.astype(_BF16)

        def merge(q_idx, k_idx, m, l, p, bi=bi):
            o = jnp.dot(p, v_s[k_idx, :].astype(_BF16), preferred_element_type=_F32)
            o2 = jnp.where(first, o[:ATT_QB], o[ATT_QB:])
            m2 = jnp.where(first, m[:ATT_QB], m[ATT_QB:])
            l2 = jnp.where(first, l[:ATT_QB], l[ATT_QB:])
            if bi == 0:
                acc_s[q_idx, :] = o2
                m_s[q_idx, :] = m2
                l_s[q_idx, :] = l2
            else:
                m_old = m_s[q_idx, :]
                m_new = jnp.maximum(m_old, m2)
                a_old = jnp.exp(m_old - m_new)
                a_new = jnp.exp(m2 - m_new)
                acc_s[q_idx, :] = acc_s[q_idx, :] * a_old + o2 * a_new
                l_s[q_idx, :] = l_s[q_idx, :] * a_old + l2 * a_new
                m_s[q_idx, :] = m_new

        def body(it, carry):
            blocks = [scores(ATT_UNROLL * it + b) for b in range(ATT_UNROLL)]
            soft = [softmax(s) for _, _, s in blocks]
            for (q_idx, k_idx, _), (m, l, p) in zip(blocks, soft):
                merge(q_idx, k_idx, m, l, p)
            return carry

        lax.fori_loop(0, chunk // ATT_QB // ATT_UNROLL, body, 0)

    o_ref[...] = (acc_s[...] / l_s[...]).astype(o_ref.dtype)


def _attention(qk, u, seq, chunk=4096):
    t = u.shape[0]
    chunk = min(chunk, seq)
    halo = ATT_HALO
    per = chunk // halo
    last = t // halo - 1
    n_hp = ATTN_WIDTH // LANES
    main = lambda off: pl.BlockSpec((chunk, LANES), lambda g, h: (g, off + h))
    left = lambda off: pl.BlockSpec((halo, LANES), lambda g, h: (jnp.maximum(g * per - 1, 0), off + h))
    right = lambda off: pl.BlockSpec((halo, LANES), lambda g, h: (jnp.minimum((g + 1) * per, last), off + h))
    v_off = 2 * ATTN_WIDTH // LANES
    kern = functools.partial(_attention_kernel, seq=seq, chunk=chunk)
    return pl.pallas_call(
        kern,
        grid=(t // chunk, n_hp),
        in_specs=[main(0), main(n_hp), left(n_hp), right(n_hp), main(v_off), left(v_off), right(v_off)],
        out_specs=pl.BlockSpec((chunk, LANES), lambda g, h: (g, h)),
        out_shape=jax.ShapeDtypeStruct((t, ATTN_WIDTH), _BF16),
        scratch_shapes=[
            pltpu.VMEM((chunk, LANES), _F32),
            pltpu.VMEM((chunk + 2 * halo, LANES), _F32),
            pltpu.VMEM((chunk + 2 * halo, LANES), _F32),
            pltpu.VMEM((chunk, LANES), _F32),
            pltpu.VMEM((chunk, LANES), _F32),
            pltpu.VMEM((chunk, LANES), _F32),
        ],
        compiler_params=_params(("parallel", "parallel")),
        name="attention",
    )(qk, qk, qk, qk, u, u, u)


CONV_PAD = CONV_KERNEL // 2
CONV_HALO = 16
CONV_RB = 64
CONV_CB = 256


def _conv_kernel(a_ref, al_ref, ar_ref, g_ref, gl_ref, gr_ref, w_ref, b_ref, lng_ref, lnb_ref, o_ref,
                 c_s, y_s, *, seq, tr):
    nblk = seq // tr
    c = pl.program_id(0) % nblk

    def glu(a, g):
        return a.astype(_F32) * jax.nn.sigmoid(g.astype(_F32))

    c_s[0:CONV_HALO, :] = glu(al_ref[...], gl_ref[...]) * (c > 0).astype(_F32)
    c_s[CONV_HALO:CONV_HALO + tr, :] = glu(a_ref[...], g_ref[...])
    c_s[CONV_HALO + tr:, :] = glu(ar_ref[...], gr_ref[...]) * (c < nblk - 1).astype(_F32)

    shift = CONV_HALO - CONV_PAD
    for cb in range(CONV_WIDTH // CONV_CB):
        cols = slice(cb * CONV_CB, (cb + 1) * CONV_CB)
        w = w_ref[:, cols]
        bias = b_ref[:, cols]

        def body(rb, carry, cols=cols, w=w, bias=bias):
            i0 = pl.multiple_of(rb * CONV_RB, CONV_RB)
            win = c_s[pl.ds(i0, CONV_RB + 2 * CONV_HALO), cols]
            acc = jnp.zeros((CONV_RB, CONV_CB), _F32) + bias
            for tap in range(CONV_KERNEL):
                acc = acc + win[tap + shift:tap + shift + CONV_RB, :] * w[tap:tap + 1, :]
            y_s[pl.ds(i0, CONV_RB), cols] = acc
            return carry

        lax.fori_loop(0, tr // CONV_RB, body, 0)

    y = y_s[...]
    mu = jnp.mean(y, axis=-1, keepdims=True)
    yc = y - mu
    var = jnp.mean(yc * yc, axis=-1, keepdims=True)
    z = yc * lax.rsqrt(var + EPS) * lng_ref[...] + lnb_ref[...]
    o_ref[...] = (z * jax.nn.sigmoid(z)).astype(o_ref.dtype)


def _conv(u, w, b, ln_g, ln_b, seq, tr=512):
    t = u.shape[0]
    per = tr // CONV_HALO
    last = t // CONV_HALO - 1
    a_blk = 3 * ATTN_WIDTH // CONV_WIDTH
    g_blk = a_blk + 1
    main = lambda col: pl.BlockSpec((tr, CONV_WIDTH), lambda i: (i, col))
    left = lambda col: pl.BlockSpec((CONV_HALO, CONV_WIDTH), lambda i: (jnp.maximum(i * per - 1, 0), col))
    right = lambda col: pl.BlockSpec((CONV_HALO, CONV_WIDTH), lambda i: (jnp.minimum((i + 1) * per, last), col))
    vec = lambda rows: pl.BlockSpec((rows, CONV_WIDTH), lambda i: (0, 0))
    kern = functools.partial(_conv_kernel, seq=seq, tr=tr)
    return pl.pallas_call(
        kern,
        grid=(t // tr,),
        in_specs=[main(a_blk), left(a_blk), right(a_blk), main(g_blk), left(g_blk), right(g_blk),
                  vec(CONV_KERNEL), vec(1), vec(1), vec(1)],
        out_specs=pl.BlockSpec((tr, CONV_WIDTH), lambda i: (i, 0)),
        out_shape=jax.ShapeDtypeStruct((t, CONV_WIDTH), _BF16),
        scratch_shapes=[pltpu.VMEM((tr + 2 * CONV_HALO, CONV_WIDTH), _F32), pltpu.VMEM((tr, CONV_WIDTH), _F32)],
        compiler_params=_params(("parallel",)),
        name="conv",
    )(u, u, u, u, u, u, w, b.reshape(1, -1), ln_g.reshape(1, -1), ln_b.reshape(1, -1))


def _rms(x, g):
    return x * lax.rsqrt(jnp.mean(x * x, axis=-1, keepdims=True) + EPS) * g


def _out_proj_kernel(x_ref, at_ref, cv_ref, ga_ref, gc_ref, w_ref, g2_ref, x1_ref, h2_ref):
    a = _rms(at_ref[...].astype(_F32), ga_ref[...]).astype(_BF16)
    c = _rms(cv_ref[...].astype(_F32), gc_ref[...]).astype(_BF16)
    mix = jnp.dot(a, w_ref[0:ATTN_WIDTH, :], preferred_element_type=_F32)
    mix = mix + jnp.dot(c, w_ref[ATTN_WIDTH:, :], preferred_element_type=_F32)
    x1 = x_ref[...] + mix
    x1_ref[...] = x1
    h2_ref[...] = _rms(x1, g2_ref[...]).astype(h2_ref.dtype)


def _out_proj(x, attn, conv, attn_g, conv_g, w_bf16, norm2_g, tm=512):
    t, d = x.shape
    row = lambda w: pl.BlockSpec((tm, w), lambda i: (i, 0))
    vec = lambda w: pl.BlockSpec((1, w), lambda i: (0, 0))
    return pl.pallas_call(
        _out_proj_kernel,
        grid=(t // tm,),
        in_specs=[row(d), row(ATTN_WIDTH), row(CONV_WIDTH), vec(ATTN_WIDTH), vec(CONV_WIDTH),
                  pl.BlockSpec((d, d), lambda i: (0, 0)), vec(d)],
        out_specs=[row(d), row(d)],
        out_shape=[jax.ShapeDtypeStruct((t, d), _F32), jax.ShapeDtypeStruct((t, d), _BF16)],
        compiler_params=_params(("parallel",)),
        name="out_proj",
    )(x, attn, conv, attn_g.reshape(1, -1), conv_g.reshape(1, -1), w_bf16, norm2_g.reshape(1, -1))


def _rank_top(work, order, rounds, exact):
    big = float(2 ** 30)
    rank = jnp.full(work.shape, float(rounds), _F32)
    values = []
    for a in range(rounds):
        m = jnp.max(work, axis=0, keepdims=True)
        sel = work == m
        if exact:
            first = jnp.min(jnp.where(sel, order, big), axis=0, keepdims=True)
            sel = order == first
        rank = jnp.where(sel, float(a), rank)
        work = jnp.where(sel, -jnp.inf, work)
        values.append(m)
    picked = jnp.sum((rank < float(rounds)).astype(_F32), axis=0, keepdims=True)
    return rank, values, picked


def _stack_rows(rows):
    sub = lax.broadcasted_iota(jnp.int32, (len(rows), rows[0].shape[1]), 0)
    out = jnp.broadcast_to(rows[-1], sub.shape)
    for i in range(len(rows) - 2, -1, -1):
        out = jnp.where(sub == i, rows[i], out)
    return out


ROUTE_NB = 8


def _route_tile(s1, s2, exact):
    k = PEER_TOPK
    key_order = lax.broadcasted_iota(jnp.int32, s1.shape, 0).astype(_F32)
    rank1, v1, picked1 = _rank_top(s1, key_order, k, exact)
    rank2, v2, picked2 = _rank_top(s2, key_order, k, exact)
    v2_lo, v2_hi = _stack_rows(v2[:8]), _stack_rows(v2[8:])
    sub8 = lax.broadcasted_iota(jnp.int32, v2_lo.shape, 0).astype(_F32)
    sums = [v1[0] + v2_lo, v1[0] + v2_hi] + [v1[a] + v2_lo for a in range(1, k)]
    order = [sub8, sub8 + 8.0] + [sub8 + float(k * a) for a in range(1, k)]
    cand = jnp.concatenate(sums, axis=0)
    cand_rank, _, picked3 = _rank_top(cand, jnp.concatenate(order, axis=0), k, exact)
    tie_free = jnp.all((picked1 == float(k)) & (picked2 == float(k)) & (picked3 == float(k)))
    chosen = (cand_rank < float(k)).astype(_F32)
    top = v1[0] + v2[0]
    z = jnp.sum(chosen * jnp.exp(cand - top), axis=0, keepdims=True)
    n = [jnp.sum(chosen[0:2 * 8], axis=0, keepdims=True)]
    for a in range(1, k):
        n.append(jnp.sum(chosen[(a + 1) * 8:(a + 2) * 8], axis=0, keepdims=True))
    n1 = jnp.zeros_like(s1)
    for a in range(k):
        n1 = jnp.where(rank1 == float(a), n[a], n1)
    c1 = jnp.exp(s1 - v1[0]) / z
    p2 = jnp.exp(s2 - v2[0])
    return rank2, p2, n1, c1, tie_free


def _route_kernel(h_ref, wq_ref, keys_ref, r2_ref, p2_ref, n1_ref, c1_ref, q_s, *, tm):
    qt = lax.dot_general(wq_ref[...], h_ref[...], (((1,), (1,)), ((), ())), preferred_element_type=_F32)
    groups = tm // LANES
    for g in range(groups):
        q_s[g] = qt[:, g * LANES:(g + 1) * LANES].astype(_BF16)

    def body(it, carry):
        h = it // groups
        g = it % groups
        row = pl.multiple_of(h * 2 * PEER_HALF_DIM, 2 * PEER_HALF_DIM)
        q1 = q_s[g, pl.ds(row, PEER_HALF_DIM), :]
        q2 = q_s[g, pl.ds(row + PEER_HALF_DIM, PEER_HALF_DIM), :]
        s1 = jnp.dot(keys_ref[2 * h], q1, preferred_element_type=_F32)
        s2 = jnp.dot(keys_ref[2 * h + 1], q2, preferred_element_type=_F32)
        def emit(rank2, p2, n1, c1):
            r2_ref[h, g] = rank2
            p2_ref[h, g] = p2
            n1_ref[h, g] = n1
            c1_ref[h, g] = c1

        *quick, tie_free = _route_tile(s1, s2, exact=False)
        emit(*quick)

        @pl.when(jnp.logical_not(tie_free))
        def _():
            emit(*_route_tile(s1, s2, exact=True)[:4])

        return carry

    lax.fori_loop(0, PEER_HEADS * groups, body, 0)


def _route(h2, wq_t_bf16, keys_bf16, tm=256):
    t, d = h2.shape
    groups = tm // LANES
    out = jax.ShapeDtypeStruct((PEER_HEADS, t // LANES, PEER_N_KEYS, LANES), _F32)
    ospec = pl.BlockSpec((PEER_HEADS, groups, PEER_N_KEYS, LANES), lambda i: (0, i, 0, 0))
    kern = functools.partial(_route_kernel, tm=tm)
    return pl.pallas_call(
        kern,
        grid=(t // tm,),
        in_specs=[
            pl.BlockSpec((tm, d), lambda i: (i, 0)),
            pl.BlockSpec(wq_t_bf16.shape, lambda i: (0, 0)),
            pl.BlockSpec(keys_bf16.shape, lambda i: (0, 0, 0)),
        ],
        out_specs=[ospec] * 4,
        out_shape=[out] * 4,
        scratch_shapes=[pltpu.VMEM((groups, wq_t_bf16.shape[0], LANES), _BF16)],
        compiler_params=_params(("parallel",)),
        name="route",
    )(h2, wq_t_bf16, keys_bf16)


def _gelu(x):
    return 0.5 * x * (1.0 + lax.erf(x * (1.0 / math.sqrt(2.0))))


EXP_TE = 1024
EXP_TOK = 256
EXP_SUB = 512
EXP_ROWS = 32
EXP_OUT = 512


def _experts_kernel(h_ref, u_ref, vt_ref, r2_ref, p2_ref, n1_ref, c1_ref, x1_ref, o_ref, acc_s, a_s, *, te):
    j = pl.program_id(1)
    tm = h_ref.shape[0]

    @pl.when(j == 0)
    def _():
        acc_s[...] = jnp.zeros_like(acc_s)

    rows = EXP_SUB // PEER_N_KEYS
    nt = (((1,), (1,)), ((), ()))

    def first(tc, ec):
        return lax.dot_general(u_ref[ec * EXP_SUB:(ec + 1) * EXP_SUB, :], h_ref[tc * EXP_TOK:(tc + 1) * EXP_TOK, :],
                               nt, preferred_element_type=_F32)

    def gates(tc, ec, act, er):
        e1 = j * (te // PEER_N_KEYS) + ec * rows + er
        for gi in range(EXP_TOK // LANES):
            g = tc * (EXP_TOK // LANES) + gi
            lanes = slice(gi * LANES, (gi + 1) * LANES)
            n1 = [n1_ref[h, g, pl.ds(e1, 1), :] for h in range(PEER_HEADS)]
            c1 = [c1_ref[h, g, pl.ds(e1, 1), :] for h in range(PEER_HEADS)]
            for sb in range(PEER_N_KEYS // EXP_ROWS):
                keys = slice(sb * EXP_ROWS, (sb + 1) * EXP_ROWS)
                gate = jnp.zeros((EXP_ROWS, LANES), _F32)
                for h in range(PEER_HEADS):
                    gate = gate + jnp.where(r2_ref[h, g, keys, :] < n1[h], p2_ref[h, g, keys, :] * c1[h], 0.0)
                src = er * PEER_N_KEYS + sb * EXP_ROWS
                dst = ec * EXP_SUB + src
                a_s[dst:dst + EXP_ROWS, tc * EXP_TOK + gi * LANES:tc * EXP_TOK + (gi + 1) * LANES] = (
                    _gelu(act[src:src + EXP_ROWS, lanes]) * gate).astype(_BF16)

    def second(tc, mc):
        tok = slice(tc * EXP_TOK, (tc + 1) * EXP_TOK)
        out_rows = slice(mc * EXP_OUT, (mc + 1) * EXP_OUT)
        acc_s[out_rows, tok] += jnp.dot(vt_ref[0, out_rows, :], a_s[:, tok], preferred_element_type=_F32)

    chains = [(tc, ec) for tc in range(tm // EXP_TOK) for ec in range(te // EXP_SUB)]
    per_tok = te // EXP_SUB
    n_out = acc_s.shape[0] // EXP_OUT
    acts = {0: first(*chains[0])}
    pending = []
    for ci, (tc, ec) in enumerate(chains):
        for er in range(rows):
            gates(tc, ec, acts[ci], er)
            if er == 0 and ci + 1 < len(chains):
                acts[ci + 1] = first(*chains[ci + 1])
            elif pending and er % 2 == 1:
                second(*pending.pop(0))
        del acts[ci]
        if ec == per_tok - 1:
            pending += [(tc, mc) for mc in range(n_out)]
    for item in pending:
        second(*item)

    @pl.when(j == pl.num_programs(1) - 1)
    def _():
        o_ref[...] = x1_ref[...] + acc_s[...].T


def _experts(h2, x1, u_bf16, vt_blocks, route, tm=512):
    t, d = h2.shape
    nj, _, te = vt_blocks.shape
    once = pl.Buffered(1)
    rspec = pl.BlockSpec((PEER_HEADS, tm // LANES, PEER_N_KEYS, LANES), lambda i, j: (0, i, 0, 0),
                         pipeline_mode=once)
    kern = functools.partial(_experts_kernel, te=te)
    return pl.pallas_call(
        kern,
        grid=(t // tm, nj),
        in_specs=[
            pl.BlockSpec((tm, d), lambda i, j: (i, 0), pipeline_mode=once),
            pl.BlockSpec((te, d), lambda i, j: (j, 0)),
            pl.BlockSpec((1, d, te), lambda i, j: (j, 0, 0)),
            rspec, rspec, rspec, rspec,
            pl.BlockSpec((tm, d), lambda i, j: (i, 0), pipeline_mode=once),
        ],
        out_specs=pl.BlockSpec((tm, d), lambda i, j: (i, 0)),
        out_shape=jax.ShapeDtypeStruct((t, d), _F32),
        scratch_shapes=[pltpu.VMEM((d, tm), _F32), pltpu.VMEM((te, tm), _BF16)],
        compiler_params=_params(("parallel", "arbitrary")),
        name="experts",
    )(h2, u_bf16, vt_blocks, *route, x1)


def _trunk(x, seq, layers, rope):
    b, s, d = x.shape
    x = x.reshape(b * s, d)
    for p in layers:
        u = _in_proj(x, p["norm1_g"], p["w_in"])
        qk = _qk_prep(u, p["q_norm_g"], p["k_norm_g"], rope, seq)
        attn = _attention(qk, u, seq)
        conv = _conv(u, p["conv_dw_w"], p["conv_dw_b"], p["conv_ln_g"], p["conv_ln_b"], seq)
        x1, h2 = _out_proj(x, attn, conv, p["attn_out_g"], p["conv_out_g"], p["w_out"], p["norm2_g"])
        route = _route(h2, p["peer_wq_t"], p["peer_keys"])
        x = _experts(h2, x1, p["peer_u"], p["peer_v_t"], route)
    return x.reshape(b, s, d)


def kernel(x_prompt, x_sample, norm1_g, w_in, q_norm_g, k_norm_g, conv_dw_w, conv_dw_b, conv_ln_g, conv_ln_b,
           attn_out_g, conv_out_g, w_out, norm2_g, peer_wq, peer_keys, peer_u, peer_v):
    depth = w_in.shape[0]
    layers = []
    for l in range(depth):
        layers.append(dict(
            norm1_g=norm1_g[l], w_in=w_in[l].astype(_BF16), q_norm_g=q_norm_g[l], k_norm_g=k_norm_g[l],
            conv_dw_w=conv_dw_w[l], conv_dw_b=conv_dw_b[l], conv_ln_g=conv_ln_g[l], conv_ln_b=conv_ln_b[l],
            attn_out_g=attn_out_g[l], conv_out_g=conv_out_g[l], w_out=w_out[l].astype(_BF16), norm2_g=norm2_g[l],
            peer_wq_t=peer_wq[l].T.astype(_BF16),
            peer_keys=peer_keys[l].reshape(2 * PEER_HEADS, PEER_N_KEYS, PEER_HALF_DIM).astype(_BF16),
            peer_u=peer_u[l].astype(_BF16),
            peer_v_t=peer_v[l].astype(_BF16).reshape(-1, EXP_TE, D_MODEL).transpose(0, 2, 1),
        ))
    outs = []
    for x in (x_prompt, x_sample):
        seq = x.shape[1]
        outs.append(_trunk(x, seq, layers, _rope_tables(seq)))
    return tuple(outs)
```

```python
import functools
import math

import jax
import jax.numpy as jnp
from jax import lax
from jax.experimental import pallas as pl
from jax.experimental.pallas import tpu as pltpu

D_MODEL = 2048
ATTN_HEADS = 16
HEAD_DIM = 64
ATTN_WIDTH = ATTN_HEADS * HEAD_DIM
CONV_WIDTH = D_MODEL - ATTN_WIDTH
CONV_KERNEL = 31
IN_WIDTH = 3 * ATTN_WIDTH + 2 * CONV_WIDTH
DILATIONS = (1, 4, 16)
HALF_WINDOW = 64
ROPE_THETA = 10000.0
NEG_INF = -1e30
PEER_HEADS = 8
PEER_N_KEYS = 128
PEER_N_EXPERTS = PEER_N_KEYS * PEER_N_KEYS
PEER_TOPK = 16
PEER_HALF_DIM = 128
EPS = 1e-6

LANES = 128
SUBLANES = 8
VMEM_LIMIT = 56 * 1024 * 1024

_BF16 = jnp.bfloat16
_F32 = jnp.float32


def _params(semantics):
    return pltpu.CompilerParams(dimension_semantics=semantics, vmem_limit_bytes=VMEM_LIMIT)


def _in_proj_kernel(x_ref, g_ref, w_ref, o_ref, h_ref):
    @pl.when(pl.program_id(1) == 0)
    def _():
        x = x_ref[...]
        ms = jnp.mean(x * x, axis=-1, keepdims=True)
        h_ref[...] = (x * lax.rsqrt(ms + EPS) * g_ref[...]).astype(_BF16)

    o_ref[...] = jnp.dot(h_ref[...], w_ref[...], preferred_element_type=_F32).astype(o_ref.dtype)


def _in_proj(x, g, w_bf16, tm=1024, tn=1024):
    t, d = x.shape
    n = w_bf16.shape[1]
    return pl.pallas_call(
        _in_proj_kernel,
        grid=(t // tm, n // tn),
        in_specs=[
            pl.BlockSpec((tm, d), lambda i, j: (i, 0)),
            pl.BlockSpec((1, d), lambda i, j: (0, 0)),
            pl.BlockSpec((d, tn), lambda i, j: (0, j)),
        ],
        out_specs=pl.BlockSpec((tm, tn), lambda i, j: (i, j)),
        out_shape=jax.ShapeDtypeStruct((t, n), _BF16),
        scratch_shapes=[pltpu.VMEM((tm, d), _BF16)],
        compiler_params=_params(("parallel", "arbitrary")),
        name="in_proj",
    )(x, g.reshape(1, d), w_bf16)


def _rope_tables(seq):
    half = HEAD_DIM // 2
    inv_freq = ROPE_THETA ** (-jnp.arange(0, HEAD_DIM, 2, dtype=_F32) / HEAD_DIM)
    ang = jnp.arange(seq, dtype=_F32)[:, None] * inv_freq[None, :]
    cos, sin = jnp.cos(ang), jnp.sin(ang)
    zero = jnp.zeros_like(sin)
    cos_t = jnp.tile(cos, (1, LANES // half))
    sin_lo = jnp.tile(jnp.concatenate([-sin, zero], axis=1), (1, LANES // HEAD_DIM))
    sin_hi = jnp.tile(jnp.concatenate([zero, sin], axis=1), (1, LANES // HEAD_DIM))
    return cos_t, sin_lo, sin_hi


def _qk_prep_kernel(u_ref, g_ref, cos_ref, slo_ref, shi_ref, o_ref):
    is_q = pl.program_id(1) < ATTN_WIDTH // LANES
    x = u_ref[...].astype(_F32)
    lane = lax.broadcasted_iota(jnp.int32, x.shape, 1)
    first = lane < HEAD_DIM
    x2 = x * x
    ss0 = jnp.sum(jnp.where(first, x2, 0.0), axis=-1, keepdims=True)
    ss1 = jnp.sum(jnp.where(first, 0.0, x2), axis=-1, keepdims=True)
    ms = jnp.where(first, ss0, ss1) * (1.0 / HEAD_DIM)
    g = jnp.where(is_q, g_ref[0:1, :], g_ref[1:2, :])
    y = x * lax.rsqrt(ms + EPS) * g
    half = HEAD_DIM // 2
    y = y * cos_ref[...] + pltpu.roll(y, LANES - half, 1) * slo_ref[...] + pltpu.roll(y, half, 1) * shi_ref[...]
    y = y * jnp.where(is_q, HEAD_DIM ** -0.5, 1.0)
    o_ref[...] = y.astype(o_ref.dtype)


def _qk_prep(u, q_g, k_g, rope, seq, tr=1024):
    t = u.shape[0]
    nblk = seq // tr
    g2 = jnp.stack([jnp.tile(q_g, LANES // HEAD_DIM), jnp.tile(k_g, LANES // HEAD_DIM)])
    tab = pl.BlockSpec((tr, LANES), lambda i, j: (i % nblk, 0))
    return pl.pallas_call(
        _qk_prep_kernel,
        grid=(t // tr, 2 * ATTN_WIDTH // LANES),
        in_specs=[
            pl.BlockSpec((tr, LANES), lambda i, j: (i, j)),
            pl.BlockSpec((2, LANES), lambda i, j: (0, 0)),
            tab, tab, tab,
        ],
        out_specs=pl.BlockSpec((tr, LANES), lambda i, j: (i, j)),
        out_shape=jax.ShapeDtypeStruct((t, 2 * ATTN_WIDTH), _BF16),
        compiler_params=_params(("parallel", "parallel")),
        name="qk_prep",
    )(u, g2, *rope)


ATT_QB = 128
ATT_W = ATT_QB + 2 * HALF_WINDOW
ATT_HALO = HALF_WINDOW * max(DILATIONS)
ATT_UNROLL = 2


def _attention_kernel(q_ref, k_ref, kl_ref, kr_ref, v_ref, vl_ref, vr_ref, o_ref,
                      q_s, k_s, v_s, acc_s, m_s, l_s, *, seq, chunk):
    nch = seq // chunk
    cpos = (pl.program_id(0) % nch) * chunk
    halo = ATT_HALO

    q_s[...] = q_ref[...].astype(_F32)
    k_s[0:halo, :] = kl_ref[...].astype(_F32)
    k_s[halo:halo + chunk, :] = k_ref[...].astype(_F32)
    k_s[halo + chunk:, :] = kr_ref[...].astype(_F32)
    v_s[0:halo, :] = vl_ref[...].astype(_F32)
    v_s[halo:halo + chunk, :] = v_ref[...].astype(_F32)
    v_s[halo + chunk:, :] = vr_ref[...].astype(_F32)

    lane = lax.broadcasted_iota(jnp.int32, (1, LANES), 1)
    first = lane < HEAD_DIM
    row2 = lax.broadcasted_iota(jnp.int32, (2 * ATT_QB, LANES), 0)
    lane2 = lax.broadcasted_iota(jnp.int32, (2 * ATT_QB, LANES), 1)
    own = ((row2 < ATT_QB) == (lane2 < HEAD_DIM)).astype(_F32)
    qi = lax.broadcasted_iota(jnp.int32, (2 * ATT_QB, ATT_W), 0) % ATT_QB
    kj = lax.broadcasted_iota(jnp.int32, (2 * ATT_QB, ATT_W), 1)
    nt = (((1,), (1,)), ((), ()))

    for bi, dil in enumerate(DILATIONS):
        nblk = chunk // dil // ATT_QB

        def scores(n, dil=dil, nblk=nblk):
            r = n // nblk
            l0 = (n % nblk) * ATT_QB
            q_start = l0 * dil + r
            k_start = halo + (l0 - HALF_WINDOW) * dil + r
            if dil == 1:
                q_idx = pl.ds(q_start, ATT_QB)
                k_idx = pl.ds(k_start, ATT_W)
            else:
                q_idx = pl.ds(q_start, ATT_QB, stride=dil)
                k_idx = pl.ds(k_start, ATT_W, stride=dil)
            q2 = q_s[q_idx, :]
            qs = (jnp.concatenate([q2, q2], axis=0) * own).astype(_BF16)
            s = lax.dot_general(qs, k_s[k_idx, :].astype(_BF16), nt, preferred_element_type=_F32)
            base = cpos + (l0 - HALF_WINDOW) * dil + r
            j_lo = (jnp.maximum(-base, 0) + dil - 1) // dil
            j_hi = (jnp.maximum(seq - base, 0) + dil - 1) // dil
            lo = jnp.maximum(qi, j_lo)
            hi = jnp.minimum(qi + 2 * HALF_WINDOW, j_hi - 1)
            return q_idx, k_idx, jnp.where((kj >= lo) & (kj <= hi), s, NEG_INF)

        def softmax(s):
            m = jnp.max(s, axis=-1, keepdims=True)
            p = jnp.exp(s - m)
            return m, jnp.sum(p, axis=-1, keepdims=True), p.astype(_BF16)

        def merge(q_idx, k_idx, m, l, p, bi=bi):
            o = jnp.dot(p, v_s[k_idx, :].astype(_BF16), preferred_element_type=_F32)
            o2 = jnp.where(first, o[:ATT_QB], o[ATT_QB:])
            m2 = jnp.where(first, m[:ATT_QB], m[ATT_QB:])
            l2 = jnp.where(first, l[:ATT_QB], l[ATT_QB:])
            if bi == 0:
                acc_s[q_idx, :] = o2
                m_s[q_idx, :] = m2
                l_s[q_idx, :] = l2
            else:
                m_old = m_s[q_idx, :]
                m_new = jnp.maximum(m_old, m2)
                a_old = jnp.exp(m_old - m_new)
                a_new = jnp.exp(m2 - m_new)
                acc_s[q_idx, :] = acc_s[q_idx, :] * a_old + o2 * a_new
                l_s[q_idx, :] = l_s[q_idx, :] * a_old + l2 * a_new
                m_s[q_idx, :] = m_new

        def body(it, carry):
            blocks = [scores(ATT_UNROLL * it + b) for b in range(ATT_UNROLL)]
            soft = [softmax(s) for _, _, s in blocks]
            for (q_idx, k_idx, _), (m, l, p) in zip(blocks, soft):
                merge(q_idx, k_idx, m, l, p)
            return carry

        lax.fori_loop(0, chunk // ATT_QB // ATT_UNROLL, body, 0)

    o_ref[...] = (acc_s[...] / l_s[...]).astype(o_ref.dtype)


def _attention(qk, u, seq, chunk=4096):
    t = u.shape[0]
    chunk = min(chunk, seq)
    halo = ATT_HALO
    per = chunk // halo
    last = t // halo - 1
    n_hp = ATTN_WIDTH // LANES
    main = lambda off: pl.BlockSpec((chunk, LANES), lambda g, h: (g, off + h))
    left = lambda off: pl.BlockSpec((halo, LANES), lambda g, h: (jnp.maximum(g * per - 1, 0), off + h))
    right = lambda off: pl.BlockSpec((halo, LANES), lambda g, h: (jnp.minimum((g + 1) * per, last), off + h))
    v_off = 2 * ATTN_WIDTH // LANES
    kern = functools.partial(_attention_kernel, seq=seq, chunk=chunk)
    return pl.pallas_call(
        kern,
        grid=(t // chunk, n_hp),
        in_specs=[main(0), main(n_hp), left(n_hp), right(n_hp), main(v_off), left(v_off), right(v_off)],
        out_specs=pl.BlockSpec((chunk, LANES), lambda g, h: (g, h)),
        out_shape=jax.ShapeDtypeStruct((t, ATTN_WIDTH), _BF16),
        scratch_shapes=[
            pltpu.VMEM((chunk, LANES), _F32),
            pltpu.VMEM((chunk + 2 * halo, LANES), _F32),
            pltpu.VMEM((chunk + 2 * halo, LANES), _F32),
            pltpu.VMEM((chunk, LANES), _F32),
            pltpu.VMEM((chunk, LANES), _F32),
            pltpu.VMEM((chunk, LANES), _F32),
        ],
        compiler_params=_params(("parallel", "parallel")),
        name="attention",
    )(qk, qk, qk, qk, u, u, u)


CONV_PAD = CONV_KERNEL // 2
CONV_HALO = 16
CONV_RB = 64
CONV_CB = 256


def _conv_kernel(a_ref, al_ref, ar_ref, g_ref, gl_ref, gr_ref, w_ref, b_ref, lng_ref, lnb_ref, o_ref,
                 c_s, y_s, *, seq, tr):
    nblk = seq // tr
    c = pl.program_id(0) % nblk

    def glu(a, g):
        return a.astype(_F32) * jax.nn.sigmoid(g.astype(_F32))

    c_s[0:CONV_HALO, :] = glu(al_ref[...], gl_ref[...]) * (c > 0).astype(_F32)
    c_s[CONV_HALO:CONV_HALO + tr, :] = glu(a_ref[...], g_ref[...])
    c_s[CONV_HALO + tr:, :] = glu(ar_ref[...], gr_ref[...]) * (c < nblk - 1).astype(_F32)

    shift = CONV_HALO - CONV_PAD
    for cb in range(CONV_WIDTH // CONV_CB):
        cols = slice(cb * CONV_CB, (cb + 1) * CONV_CB)
        w = w_ref[:, cols]
        bias = b_ref[:, cols]

        def body(rb, carry, cols=cols, w=w, bias=bias):
            i0 = pl.multiple_of(rb * CONV_RB, CONV_RB)
            win = c_s[pl.ds(i0, CONV_RB + 2 * CONV_HALO), cols]
            acc = jnp.zeros((CONV_RB, CONV_CB), _F32) + bias
            for phase in range(SUBLANES):
                shifted = pltpu.roll(win, win.shape[0] - phase, 0) if phase else win
                for off in range(phase, CONV_KERNEL + shift, SUBLANES):
                    tap = off - shift
                    if 0 <= tap < CONV_KERNEL:
                        acc = acc + shifted[off - phase:off - phase + CONV_RB, :] * w[tap:tap + 1, :]
            y_s[pl.ds(i0, CONV_RB), cols] = acc
            return carry

        lax.fori_loop(0, tr // CONV_RB, body, 0)

    y = y_s[...]
    mu = jnp.mean(y, axis=-1, keepdims=True)
    yc = y - mu
    var = jnp.mean(yc * yc, axis=-1, keepdims=True)
    z = yc * lax.rsqrt(var + EPS) * lng_ref[...] + lnb_ref[...]
    o_ref[...] = (z * jax.nn.sigmoid(z)).astype(o_ref.dtype)


def _conv(u, w, b, ln_g, ln_b, seq, tr=512):
    t = u.shape[0]
    per = tr // CONV_HALO
    last = t // CONV_HALO - 1
    a_blk = 3 * ATTN_WIDTH // CONV_WIDTH
    g_blk = a_blk + 1
    main = lambda col: pl.BlockSpec((tr, CONV_WIDTH), lambda i: (i, col))
    left = lambda col: pl.BlockSpec((CONV_HALO, CONV_WIDTH), lambda i: (jnp.maximum(i * per - 1, 0), col))
    right = lambda col: pl.BlockSpec((CONV_HALO, CONV_WIDTH), lambda i: (jnp.minimum((i + 1) * per, last), col))
    vec = lambda rows: pl.BlockSpec((rows, CONV_WIDTH), lambda i: (0, 0))
    kern = functools.partial(_conv_kernel, seq=seq, tr=tr)
    return pl.pallas_call(
        kern,
        grid=(t // tr,),
        in_specs=[main(a_blk), left(a_blk), right(a_blk), main(g_blk), left(g_blk), right(g_blk),
                  vec(CONV_KERNEL), vec(1), vec(1), vec(1)],
        out_specs=pl.BlockSpec((tr, CONV_WIDTH), lambda i: (i, 0)),
        out_shape=jax.ShapeDtypeStruct((t, CONV_WIDTH), _BF16),
        scratch_shapes=[pltpu.VMEM((tr + 2 * CONV_HALO, CONV_WIDTH), _F32), pltpu.VMEM((tr, CONV_WIDTH), _F32)],
        compiler_params=_params(("parallel",)),
        name="conv",
    )(u, u, u, u, u, u, w, b.reshape(1, -1), ln_g.reshape(1, -1), ln_b.reshape(1, -1))


def _rms(x, g):
    return x * lax.rsqrt(jnp.mean(x * x, axis=-1, keepdims=True) + EPS) * g


def _out_proj_kernel(x_ref, at_ref, cv_ref, ga_ref, gc_ref, w_ref, g2_ref, x1_ref, h2_ref):
    a = _rms(at_ref[...].astype(_F32), ga_ref[...]).astype(_BF16)
    c = _rms(cv_ref[...].astype(_F32), gc_ref[...]).astype(_BF16)
    mix = jnp.dot(a, w_ref[0:ATTN_WIDTH, :], preferred_element_type=_F32)
    mix = mix + jnp.dot(c, w_ref[ATTN_WIDTH:, :], preferred_element_type=_F32)
    x1 = x_ref[...] + mix
    x1_ref[...] = x1
    h2_ref[...] = _rms(x1, g2_ref[...]).astype(h2_ref.dtype)


def _out_proj(x, attn, conv, attn_g, conv_g, w_bf16, norm2_g, tm=512):
    t, d = x.shape
    row = lambda w: pl.BlockSpec((tm, w), lambda i: (i, 0))
    vec = lambda w: pl.BlockSpec((1, w), lambda i: (0, 0))
    return pl.pallas_call(
        _out_proj_kernel,
        grid=(t // tm,),
        in_specs=[row(d), row(ATTN_WIDTH), row(CONV_WIDTH), vec(ATTN_WIDTH), vec(CONV_WIDTH),
                  pl.BlockSpec((d, d), lambda i: (0, 0)), vec(d)],
        out_specs=[row(d), row(d)],
        out_shape=[jax.ShapeDtypeStruct((t, d), _F32), jax.ShapeDtypeStruct((t, d), _BF16)],
        compiler_params=_params(("parallel",)),
        name="out_proj",
    )(x, attn, conv, attn_g.reshape(1, -1), conv_g.reshape(1, -1), w_bf16, norm2_g.reshape(1, -1))


def _rank_top(work, order, rounds, exact):
    (rank, values, picked), = _rank_top_many([work], [order], rounds, exact)
    return rank, values, picked


def _rank_top_many(works, orders, rounds, exact):
    big = float(2 ** 30)
    works = list(works)
    ranks = [jnp.full(w.shape, float(rounds), _F32) for w in works]
    values = [[] for _ in works]
    for a in range(rounds):
        for i, order in enumerate(orders):
            m = jnp.max(works[i], axis=0, keepdims=True)
            sel = works[i] == m
            if exact:
                first = jnp.min(jnp.where(sel, order, big), axis=0, keepdims=True)
                sel = order == first
            ranks[i] = jnp.where(sel, float(a), ranks[i])
            works[i] = jnp.where(sel, -jnp.inf, works[i])
            values[i].append(m)
    picked = [jnp.sum((r < float(rounds)).astype(_F32), axis=0, keepdims=True) for r in ranks]
    return list(zip(ranks, values, picked))


def _stack_rows(rows):
    sub = lax.broadcasted_iota(jnp.int32, (len(rows), rows[0].shape[1]), 0)
    out = jnp.broadcast_to(rows[-1], sub.shape)
    for i in range(len(rows) - 2, -1, -1):
        out = jnp.where(sub == i, rows[i], out)
    return out


ROUTE_NB = 8


def _route_tile(s1, s2, exact):
    k = PEER_TOPK
    key_order = lax.broadcasted_iota(jnp.int32, s1.shape, 0).astype(_F32)
    (rank1, v1, picked1), (rank2, v2, picked2) = _rank_top_many([s1, s2], [key_order, key_order], k, exact)
    v2_lo, v2_hi = _stack_rows(v2[:8]), _stack_rows(v2[8:])
    sub8 = lax.broadcasted_iota(jnp.int32, v2_lo.shape, 0).astype(_F32)
    sums = [v1[0] + v2_lo, v1[0] + v2_hi] + [v1[a] + v2_lo for a in range(1, k)]
    order = [sub8, sub8 + 8.0] + [sub8 + float(k * a) for a in range(1, k)]
    cand = jnp.concatenate(sums, axis=0)
    cand_rank, _, picked3 = _rank_top(cand, jnp.concatenate(order, axis=0), k, exact)
    tie_free = jnp.all((picked1 == float(k)) & (picked2 == float(k)) & (picked3 == float(k)))
    chosen = (cand_rank < float(k)).astype(_F32)
    top = v1[0] + v2[0]
    z = jnp.sum(chosen * jnp.exp(cand - top), axis=0, keepdims=True)
    n = [jnp.sum(chosen[0:2 * 8], axis=0, keepdims=True)]
    for a in range(1, k):
        n.append(jnp.sum(chosen[(a + 1) * 8:(a + 2) * 8], axis=0, keepdims=True))
    n1 = jnp.zeros_like(s1)
    for a in range(k):
        n1 = jnp.where(rank1 == float(a), n[a], n1)
    c1 = jnp.exp(s1 - v1[0]) / z
    p2 = jnp.exp(s2 - v2[0])
    return rank2, p2, n1, c1, tie_free


def _twice_bf16(x):
    bits = pltpu.bitcast(x.astype(_BF16).astype(_F32), jnp.uint32) >> 16
    return bits | (bits << 16)


def _route_kernel(h_ref, wq_ref, keys_ref, r2_ref, p2_ref, n1_ref, c1_ref, s_s, *, tm):
    qt = lax.dot_general(wq_ref[...], h_ref[...], (((1,), (1,)), ((), ())), preferred_element_type=_F32)
    groups = tm // LANES
    for hc in range(2 * PEER_HEADS):
        q_hc = qt[hc * PEER_HALF_DIM:(hc + 1) * PEER_HALF_DIM, :].astype(_BF16)
        s_hc = jnp.dot(keys_ref[hc], q_hc, preferred_element_type=_F32)
        for g in range(groups):
            s_s[hc, g] = s_hc[:, g * LANES:(g + 1) * LANES]

    def body(it, carry):
        h = it // groups
        g = it % groups
        s1 = s_s[2 * h, g]
        s2 = s_s[2 * h + 1, g]

        def emit(rank2, p2, n1, c1):
            r2_ref[h, g] = pltpu.bitcast(rank2.astype(_BF16), jnp.uint32)
            p2_ref[h, g] = pltpu.bitcast(p2.astype(_BF16), jnp.uint32)
            n1_ref[h, g] = _twice_bf16(n1)
            c1_ref[h, g] = _twice_bf16(c1)

        *quick, tie_free = _route_tile(s1, s2, exact=False)
        emit(*quick)

        @pl.when(jnp.logical_not(tie_free))
        def _():
            emit(*_route_tile(s1, s2, exact=True)[:4])

        return carry

    lax.fori_loop(0, PEER_HEADS * groups, body, 0)


def _route(h2, wq_t_bf16, keys_bf16, tm=256):
    t, d = h2.shape
    groups = tm // LANES
    shapes = [(PEER_N_KEYS // 2, LANES)] * 2 + [(PEER_N_KEYS, LANES)] * 2
    out = [jax.ShapeDtypeStruct((PEER_HEADS, t // LANES) + s, jnp.uint32) for s in shapes]
    ospec = [pl.BlockSpec((PEER_HEADS, groups) + s, lambda i: (0, i, 0, 0)) for s in shapes]
    kern = functools.partial(_route_kernel, tm=tm)
    return pl.pallas_call(
        kern,
        grid=(t // tm,),
        in_specs=[
            pl.BlockSpec((tm, d), lambda i: (i, 0)),
            pl.BlockSpec(wq_t_bf16.shape, lambda i: (0, 0)),
            pl.BlockSpec(keys_bf16.shape, lambda i: (0, 0, 0)),
        ],
        out_specs=ospec,
        out_shape=out,
        scratch_shapes=[pltpu.VMEM((2 * PEER_HEADS, groups, PEER_N_KEYS, LANES), _F32)],
        compiler_params=_params(("parallel",)),
        name="route",
    )(h2, wq_t_bf16, keys_bf16)


def _gelu(x):
    return 0.5 * x * (1.0 + lax.erf(x * (1.0 / math.sqrt(2.0))))


EXP_TE = 1024
EXP_TOK = 256
EXP_SUB = 512
EXP_ROWS = 16
EXP_OUT = 512
EXP_E = 1


def _experts_kernel(h_ref, u_ref, vt_ref, r2_ref, p2_ref, n1_ref, c1_ref, x1_ref, o_ref, acc_s, a_s, *, te):
    j = pl.program_id(1)
    tm = h_ref.shape[0]

    @pl.when(j == 0)
    def _():
        acc_s[...] = jnp.zeros_like(acc_s)

    rows = EXP_SUB // PEER_N_KEYS
    nt = (((1,), (1,)), ((), ()))

    def first(tc, ec):
        return lax.dot_general(u_ref[ec * EXP_SUB:(ec + 1) * EXP_SUB, :], h_ref[tc * EXP_TOK:(tc + 1) * EXP_TOK, :],
                               nt, preferred_element_type=_F32)

    def gates(tc, ec, act, piece):
        ers = range(piece * EXP_E, (piece + 1) * EXP_E)
        e1s = [j * (te // PEER_N_KEYS) + ec * rows + er for er in ers]
        for gi in range(EXP_TOK // LANES):
            g = tc * (EXP_TOK // LANES) + gi
            lanes = slice(gi * LANES, (gi + 1) * LANES)
            wide = lambda ref, h, e1: pltpu.bitcast(
                jnp.broadcast_to(ref[h, g, pl.ds(e1, 1), :], (EXP_ROWS // 2, LANES)), _BF16)
            n1 = [[wide(n1_ref, h, e1) for h in range(PEER_HEADS)] for e1 in e1s]
            c1 = [[wide(c1_ref, h, e1) for h in range(PEER_HEADS)] for e1 in e1s]
            for sb in range(PEER_N_KEYS // EXP_ROWS):
                words = slice(sb * EXP_ROWS // 2, (sb + 1) * EXP_ROWS // 2)
                gate = [jnp.zeros((EXP_ROWS, LANES), _BF16) for _ in ers]
                for h in range(PEER_HEADS):
                    r2 = pltpu.bitcast(r2_ref[h, g, words, :], _BF16)
                    p2 = pltpu.bitcast(p2_ref[h, g, words, :], _BF16)
                    for k in range(len(e1s)):
                        gate[k] = gate[k] + jnp.where(r2 < n1[k][h], p2 * c1[k][h], jnp.zeros((), _BF16))
                for k, er in enumerate(ers):
                    src = er * PEER_N_KEYS + sb * EXP_ROWS
                    dst = ec * EXP_SUB + src
                    a_s[dst:dst + EXP_ROWS, tc * EXP_TOK + gi * LANES:tc * EXP_TOK + (gi + 1) * LANES] = (
                        _gelu(act[src:src + EXP_ROWS, lanes]).astype(_BF16) * gate[k])

    def second(tc, mc):
        tok = slice(tc * EXP_TOK, (tc + 1) * EXP_TOK)
        out_rows = slice(mc * EXP_OUT, (mc + 1) * EXP_OUT)
        acc_s[out_rows, tok] += jnp.dot(vt_ref[0, out_rows, :], a_s[:, tok], preferred_element_type=_F32)

    chains = [(tc, ec) for tc in range(tm // EXP_TOK) for ec in range(te // EXP_SUB)]
    per_tok = te // EXP_SUB
    n_out = acc_s.shape[0] // EXP_OUT
    acts = {0: first(*chains[0])}
    pending = []
    pieces = rows // EXP_E
    for ci, (tc, ec) in enumerate(chains):
        for piece in range(pieces):
            gates(tc, ec, acts[ci], piece)
            if piece == 0 and ci + 1 < len(chains):
                acts[ci + 1] = first(*chains[ci + 1])
            for _ in range(n_out // len(chains) if pending else 0):
                second(*pending.pop(0))
        del acts[ci]
        if ec == per_tok - 1:
            pending += [(tc, mc) for mc in range(n_out)]
    for item in pending:
        second(*item)

    @pl.when(j == pl.num_programs(1) - 1)
    def _():
        o_ref[...] = x1_ref[...] + acc_s[...].T


def _experts(h2, x1, u_bf16, vt_blocks, route, tm=512):
    t, d = h2.shape
    nj, _, te = vt_blocks.shape
    once = pl.Buffered(1)
    rspec = [pl.BlockSpec((PEER_HEADS, tm // LANES) + r.shape[2:], lambda i, j: (0, i, 0, 0), pipeline_mode=once)
             for r in route]
    kern = functools.partial(_experts_kernel, te=te)
    return pl.pallas_call(
        kern,
        grid=(t // tm, nj),
        in_specs=[
            pl.BlockSpec((tm, d), lambda i, j: (i, 0), pipeline_mode=once),
            pl.BlockSpec((te, d), lambda i, j: (j, 0)),
            pl.BlockSpec((1, d, te), lambda i, j: (j, 0, 0)),
            *rspec,
            pl.BlockSpec((tm, d), lambda i, j: (i, 0), pipeline_mode=once),
        ],
        out_specs=pl.BlockSpec((tm, d), lambda i, j: (i, 0)),
        out_shape=jax.ShapeDtypeStruct((t, d), _F32),
        scratch_shapes=[pltpu.VMEM((d, tm), _F32), pltpu.VMEM((te, tm), _BF16)],
        compiler_params=_params(("parallel", "arbitrary")),
        name="experts",
    )(h2, u_bf16, vt_blocks, *route, x1)


def _trunk(x, seq, layers, rope):
    b, s, d = x.shape
    x = x.reshape(b * s, d)
    for p in layers:
        u = _in_proj(x, p["norm1_g"], p["w_in"])
        qk = _qk_prep(u, p["q_norm_g"], p["k_norm_g"], rope, seq)
        attn = _attention(qk, u, seq)
        conv = _conv(u, p["conv_dw_w"], p["conv_dw_b"], p["conv_ln_g"], p["conv_ln_b"], seq)
        x1, h2 = _out_proj(x, attn, conv, p["attn_out_g"], p["conv_out_g"], p["w_out"], p["norm2_g"])
        route = _route(h2, p["peer_wq_t"], p["peer_keys"])
        x = _experts(h2, x1, p["peer_u"], p["peer_v_t"], route)
    return x.reshape(b, s, d)


def kernel(x_prompt, x_sample, norm1_g, w_in, q_norm_g, k_norm_g, conv_dw_w, conv_dw_b, conv_ln_g, conv_ln_b,
           attn_out_g, conv_out_g, w_out, norm2_g, peer_wq, peer_keys, peer_u, peer_v):
    depth = w_in.shape[0]
    layers = []
    for l in range(depth):
        layers.append(dict(
            norm1_g=norm1_g[l], w_in=w_in[l].astype(_BF16), q_norm_g=q_norm_g[l], k_norm_g=k_norm_g[l],
            conv_dw_w=conv_dw_w[l], conv_dw_b=conv_dw_b[l], conv_ln_g=conv_ln_g[l], conv_ln_b=conv_ln_b[l],
            attn_out_g=attn_out_g[l], conv_out_g=conv_out_g[l], w_out=w_out[l].astype(_BF16), norm2_g=norm2_g[l],
            peer_wq_t=peer_wq[l].T.astype(_BF16),
            peer_keys=peer_keys[l].reshape(2 * PEER_HEADS, PEER_N_KEYS, PEER_HALF_DIM).astype(_BF16),
            peer_u=peer_u[l].astype(_BF16),
            peer_v_t=peer_v[l].astype(_BF16).reshape(-1, EXP_TE, D_MODEL).transpose(0, 2, 1),
        ))
    outs = []
    for x in (x_prompt, x_sample):
        seq = x.shape[1]
        outs.append(_trunk(x, seq, layers, _rope_tables(seq)))
    return tuple(outs)
```

```python
import functools
import math

import jax
import jax.numpy as jnp
from jax import lax
from jax.experimental import pallas as pl
from jax.experimental.pallas import tpu as pltpu

D_MODEL = 2048
ATTN_HEADS = 16
HEAD_DIM = 64
ATTN_WIDTH = ATTN_HEADS * HEAD_DIM
CONV_WIDTH = D_MODEL - ATTN_WIDTH
CONV_KERNEL = 31
IN_WIDTH = 3 * ATTN_WIDTH + 2 * CONV_WIDTH
DILATIONS = (1, 4, 16)
HALF_WINDOW = 64
ROPE_THETA = 10000.0
NEG_INF = -1e30
PEER_HEADS = 8
PEER_N_KEYS = 128
PEER_N_EXPERTS = PEER_N_KEYS * PEER_N_KEYS
PEER_TOPK = 16
PEER_HALF_DIM = 128
EPS = 1e-6

LANES = 128
SUBLANES = 8
VMEM_LIMIT = 56 * 1024 * 1024

_BF16 = jnp.bfloat16
_F32 = jnp.float32


def _params(semantics):
    return pltpu.CompilerParams(dimension_semantics=semantics, vmem_limit_bytes=VMEM_LIMIT)


def _in_proj_kernel(x_ref, g_ref, w_ref, o_ref, h_ref):
    @pl.when(pl.program_id(1) == 0)
    def _():
        x = x_ref[...]
        ms = jnp.mean(x * x, axis=-1, keepdims=True)
        h_ref[...] = (x * lax.rsqrt(ms + EPS) * g_ref[...]).astype(_BF16)

    o_ref[...] = jnp.dot(h_ref[...], w_ref[...], preferred_element_type=_F32).astype(o_ref.dtype)


def _in_proj(x, g, w_bf16, tm=1024, tn=1024):
    t, d = x.shape
    n = w_bf16.shape[1]
    return pl.pallas_call(
        _in_proj_kernel,
        grid=(t // tm, n // tn),
        in_specs=[
            pl.BlockSpec((tm, d), lambda i, j: (i, 0)),
            pl.BlockSpec((1, d), lambda i, j: (0, 0)),
            pl.BlockSpec((d, tn), lambda i, j: (0, j)),
        ],
        out_specs=pl.BlockSpec((tm, tn), lambda i, j: (i, j)),
        out_shape=jax.ShapeDtypeStruct((t, n), _BF16),
        scratch_shapes=[pltpu.VMEM((tm, d), _BF16)],
        compiler_params=_params(("parallel", "arbitrary")),
        name="in_proj",
    )(x, g.reshape(1, d), w_bf16)


def _rope_tables(seq):
    half = HEAD_DIM // 2
    inv_freq = ROPE_THETA ** (-jnp.arange(0, HEAD_DIM, 2, dtype=_F32) / HEAD_DIM)
    ang = jnp.arange(seq, dtype=_F32)[:, None] * inv_freq[None, :]
    cos, sin = jnp.cos(ang), jnp.sin(ang)
    zero = jnp.zeros_like(sin)
    cos_t = jnp.tile(cos, (1, LANES // half))
    sin_lo = jnp.tile(jnp.concatenate([-sin, zero], axis=1), (1, LANES // HEAD_DIM))
    sin_hi = jnp.tile(jnp.concatenate([zero, sin], axis=1), (1, LANES // HEAD_DIM))
    return cos_t, sin_lo, sin_hi


def _qk_prep_kernel(u_ref, g_ref, cos_ref, slo_ref, shi_ref, o_ref):
    is_q = pl.program_id(1) < ATTN_WIDTH // LANES
    x = u_ref[...].astype(_F32)
    lane = lax.broadcasted_iota(jnp.int32, x.shape, 1)
    first = lane < HEAD_DIM
    x2 = x * x
    ss0 = jnp.sum(jnp.where(first, x2, 0.0), axis=-1, keepdims=True)
    ss1 = jnp.sum(jnp.where(first, 0.0, x2), axis=-1, keepdims=True)
    ms = jnp.where(first, ss0, ss1) * (1.0 / HEAD_DIM)
    g = jnp.where(is_q, g_ref[0:1, :], g_ref[1:2, :])
    y = x * lax.rsqrt(ms + EPS) * g
    half = HEAD_DIM // 2
    y = y * cos_ref[...] + pltpu.roll(y, LANES - half, 1) * slo_ref[...] + pltpu.roll(y, half, 1) * shi_ref[...]
    y = y * jnp.where(is_q, HEAD_DIM ** -0.5, 1.0)
    o_ref[...] = y.astype(o_ref.dtype)


def _qk_prep(u, q_g, k_g, rope, seq, tr=1024):
    t = u.shape[0]
    nblk = seq // tr
    g2 = jnp.stack([jnp.tile(q_g, LANES // HEAD_DIM), jnp.tile(k_g, LANES // HEAD_DIM)])
    tab = pl.BlockSpec((tr, LANES), lambda i, j: (i % nblk, 0))
    return pl.pallas_call(
        _qk_prep_kernel,
        grid=(t // tr, 2 * ATTN_WIDTH // LANES),
        in_specs=[
            pl.BlockSpec((tr, LANES), lambda i, j: (i, j)),
            pl.BlockSpec((2, LANES), lambda i, j: (0, 0)),
            tab, tab, tab,
        ],
        out_specs=pl.BlockSpec((tr, LANES), lambda i, j: (i, j)),
        out_shape=jax.ShapeDtypeStruct((t, 2 * ATTN_WIDTH), _BF16),
        compiler_params=_params(("parallel", "parallel")),
        name="qk_prep",
    )(u, g2, *rope)


ATT_QB = 128
ATT_W = ATT_QB + 2 * HALF_WINDOW
ATT_HALO = HALF_WINDOW * max(DILATIONS)
ATT_UNROLL = 4


def _attention_kernel(q_ref, k_ref, kl_ref, kr_ref, v_ref, vl_ref, vr_ref, o_ref,
                      q_s, k_s, v_s, acc_s, m_s, l_s, *, seq, chunk):
    nch = seq // chunk
    cpos = (pl.program_id(0) % nch) * chunk
    halo = ATT_HALO

    q_s[...] = q_ref[...].astype(_F32)
    k_s[0:halo, :] = kl_ref[...].astype(_F32)
    k_s[halo:halo + chunk, :] = k_ref[...].astype(_F32)
    k_s[halo + chunk:, :] = kr_ref[...].astype(_F32)
    v_s[0:halo, :] = vl_ref[...].astype(_F32)
    v_s[halo:halo + chunk, :] = v_ref[...].astype(_F32)
    v_s[halo + chunk:, :] = vr_ref[...].astype(_F32)

    lane = lax.broadcasted_iota(jnp.int32, (1, LANES), 1)
    first = lane < HEAD_DIM
    row2 = lax.broadcasted_iota(jnp.int32, (2 * ATT_QB, LANES), 0)
    lane2 = lax.broadcasted_iota(jnp.int32, (2 * ATT_QB, LANES), 1)
    own = ((row2 < ATT_QB) == (lane2 < HEAD_DIM)).astype(_F32)
    qi = lax.broadcasted_iota(jnp.int32, (2 * ATT_QB, ATT_W), 0) % ATT_QB
    kj = lax.broadcasted_iota(jnp.int32, (2 * ATT_QB, ATT_W), 1)
    nt = (((1,), (1,)), ((), ()))

    for bi, dil in enumerate(DILATIONS):
        nblk = chunk // dil // ATT_QB

        def scores(n, dil=dil, nblk=nblk):
            r = n // nblk
            l0 = (n % nblk) * ATT_QB
            q_start = l0 * dil + r
            k_start = halo + (l0 - HALF_WINDOW) * dil + r
            if dil == 1:
                q_idx = pl.ds(q_start, ATT_QB)
                k_idx = pl.ds(k_start, ATT_W)
            else:
                q_idx = pl.ds(q_start, ATT_QB, stride=dil)
                k_idx = pl.ds(k_start, ATT_W, stride=dil)
            q2 = q_s[q_idx, :]
            qs = (jnp.concatenate([q2, q2], axis=0) * own).astype(_BF16)
            s = lax.dot_general(qs, k_s[k_idx, :].astype(_BF16), nt, preferred_element_type=_F32)
            base = cpos + (l0 - HALF_WINDOW) * dil + r
            j_lo = (jnp.maximum(-base, 0) + dil - 1) // dil
            j_hi = (jnp.maximum(seq - base, 0) + dil - 1) // dil
            lo = jnp.maximum(qi, j_lo)
            hi = jnp.minimum(qi + 2 * HALF_WINDOW, j_hi - 1)
            return q_idx, k_idx, jnp.where((kj >= lo) & (kj <= hi), s, NEG_INF)

        def softmax(s):
            m = jnp.max(s, axis=-1, keepdims=True)
            p = jnp.exp(s - m)
            return m, jnp.sum(p, axis=-1, keepdims=True), p.astype(_BF16)

        def merge(q_idx, k_idx, m, l, p, bi=bi):
            o = jnp.dot(p, v_s[k_idx, :].astype(_BF16), preferred_element_type=_F32)
            o2 = jnp.where(first, o[:ATT_QB], o[ATT_QB:])
            m2 = jnp.where(first, m[:ATT_QB], m[ATT_QB:])
            l2 = jnp.where(first, l[:ATT_QB], l[ATT_QB:])
            if bi == 0:
                acc_s[q_idx, :] = o2
                m_s[q_idx, :] = m2
                l_s[q_idx, :] = l2
            else:
                m_old = m_s[q_idx, :]
                m_new = jnp.maximum(m_old, m2)
                a_old = jnp.exp(m_old - m_new)
                a_new = jnp.exp(m2 - m_new)
                acc_s[q_idx, :] = acc_s[q_idx, :] * a_old + o2 * a_new
                l_s[q_idx, :] = l_s[q_idx, :] * a_old + l2 * a_new
                m_s[q_idx, :] = m_new

        def body(it, carry):
            blocks = [scores(ATT_UNROLL * it + b) for b in range(ATT_UNROLL)]
            soft = [softmax(s) for _, _, s in blocks]
            for (q_idx, k_idx, _), (m, l, p) in zip(blocks, soft):
                merge(q_idx, k_idx, m, l, p)
            return carry

        lax.fori_loop(0, chunk // ATT_QB // ATT_UNROLL, body, 0)

    o_ref[...] = (acc_s[...] / l_s[...]).astype(o_ref.dtype)


def _attention(qk, u, seq, chunk=4096):
    t = u.shape[0]
    chunk = min(chunk, seq)
    halo = ATT_HALO
    per = chunk // halo
    last = t // halo - 1
    n_hp = ATTN_WIDTH // LANES
    main = lambda off: pl.BlockSpec((chunk, LANES), lambda g, h: (g, off + h))
    left = lambda off: pl.BlockSpec((halo, LANES), lambda g, h: (jnp.maximum(g * per - 1, 0), off + h))
    right = lambda off: pl.BlockSpec((halo, LANES), lambda g, h: (jnp.minimum((g + 1) * per, last), off + h))
    v_off = 2 * ATTN_WIDTH // LANES
    kern = functools.partial(_attention_kernel, seq=seq, chunk=chunk)
    return pl.pallas_call(
        kern,
        grid=(t // chunk, n_hp),
        in_specs=[main(0), main(n_hp), left(n_hp), right(n_hp), main(v_off), left(v_off), right(v_off)],
        out_specs=pl.BlockSpec((chunk, LANES), lambda g, h: (g, h)),
        out_shape=jax.ShapeDtypeStruct((t, ATTN_WIDTH), _BF16),
        scratch_shapes=[
            pltpu.VMEM((chunk, LANES), _F32),
            pltpu.VMEM((chunk + 2 * halo, LANES), _F32),
            pltpu.VMEM((chunk + 2 * halo, LANES), _F32),
            pltpu.VMEM((chunk, LANES), _F32),
            pltpu.VMEM((chunk, LANES), _F32),
            pltpu.VMEM((chunk, LANES), _F32),
        ],
        compiler_params=_params(("parallel", "parallel")),
        name="attention",
    )(qk, qk, qk, qk, u, u, u)


CONV_PAD = CONV_KERNEL // 2
CONV_HALO = 16
CONV_RB = 64
CONV_CB = 256


def _conv_kernel(a_ref, al_ref, ar_ref, g_ref, gl_ref, gr_ref, w_ref, b_ref, lng_ref, lnb_ref, o_ref,
                 c_s, y_s, *, seq, tr):
    nblk = seq // tr
    c = pl.program_id(0) % nblk

    def glu(a, g):
        return a.astype(_F32) * jax.nn.sigmoid(g.astype(_F32))

    c_s[0:CONV_HALO, :] = glu(al_ref[...], gl_ref[...]) * (c > 0).astype(_F32)
    c_s[CONV_HALO:CONV_HALO + tr, :] = glu(a_ref[...], g_ref[...])
    c_s[CONV_HALO + tr:, :] = glu(ar_ref[...], gr_ref[...]) * (c < nblk - 1).astype(_F32)

    shift = CONV_HALO - CONV_PAD
    for cb in range(CONV_WIDTH // CONV_CB):
        cols = slice(cb * CONV_CB, (cb + 1) * CONV_CB)
        w = w_ref[:, cols]
        bias = b_ref[:, cols]

        def body(rb, carry, cols=cols, w=w, bias=bias):
            i0 = pl.multiple_of(rb * CONV_RB, CONV_RB)
            win = c_s[pl.ds(i0, CONV_RB + 2 * CONV_HALO), cols]
            acc = jnp.zeros((CONV_RB, CONV_CB), _F32) + bias
            for phase in range(SUBLANES):
                shifted = pltpu.roll(win, win.shape[0] - phase, 0) if phase else win
                for off in range(phase, CONV_KERNEL + shift, SUBLANES):
                    tap = off - shift
                    if 0 <= tap < CONV_KERNEL:
                        acc = acc + shifted[off - phase:off - phase + CONV_RB, :] * w[tap:tap + 1, :]
            y_s[pl.ds(i0, CONV_RB), cols] = acc
            return carry

        lax.fori_loop(0, tr // CONV_RB, body, 0)

    y = y_s[...]
    mu = jnp.mean(y, axis=-1, keepdims=True)
    yc = y - mu
    var = jnp.mean(yc * yc, axis=-1, keepdims=True)
    z = yc * lax.rsqrt(var + EPS) * lng_ref[...] + lnb_ref[...]
    o_ref[...] = (z * jax.nn.sigmoid(z)).astype(o_ref.dtype)


def _conv(u, w, b, ln_g, ln_b, seq, tr=512):
    t = u.shape[0]
    per = tr // CONV_HALO
    last = t // CONV_HALO - 1
    a_blk = 3 * ATTN_WIDTH // CONV_WIDTH
    g_blk = a_blk + 1
    main = lambda col: pl.BlockSpec((tr, CONV_WIDTH), lambda i: (i, col))
    left = lambda col: pl.BlockSpec((CONV_HALO, CONV_WIDTH), lambda i: (jnp.maximum(i * per - 1, 0), col))
    right = lambda col: pl.BlockSpec((CONV_HALO, CONV_WIDTH), lambda i: (jnp.minimum((i + 1) * per, last), col))
    vec = lambda rows: pl.BlockSpec((rows, CONV_WIDTH), lambda i: (0, 0))
    kern = functools.partial(_conv_kernel, seq=seq, tr=tr)
    return pl.pallas_call(
        kern,
        grid=(t // tr,),
        in_specs=[main(a_blk), left(a_blk), right(a_blk), main(g_blk), left(g_blk), right(g_blk),
                  vec(CONV_KERNEL), vec(1), vec(1), vec(1)],
        out_specs=pl.BlockSpec((tr, CONV_WIDTH), lambda i: (i, 0)),
        out_shape=jax.ShapeDtypeStruct((t, CONV_WIDTH), _BF16),
        scratch_shapes=[pltpu.VMEM((tr + 2 * CONV_HALO, CONV_WIDTH), _F32), pltpu.VMEM((tr, CONV_WIDTH), _F32)],
        compiler_params=_params(("parallel",)),
        name="conv",
    )(u, u, u, u, u, u, w, b.reshape(1, -1), ln_g.reshape(1, -1), ln_b.reshape(1, -1))


def _rms(x, g):
    return x * lax.rsqrt(jnp.mean(x * x, axis=-1, keepdims=True) + EPS) * g


def _out_proj_kernel(x_ref, at_ref, cv_ref, ga_ref, gc_ref, w_ref, g2_ref, x1_ref, h2_ref):
    a = _rms(at_ref[...].astype(_F32), ga_ref[...]).astype(_BF16)
    c = _rms(cv_ref[...].astype(_F32), gc_ref[...]).astype(_BF16)
    mix = jnp.dot(a, w_ref[0:ATTN_WIDTH, :], preferred_element_type=_F32)
    mix = mix + jnp.dot(c, w_ref[ATTN_WIDTH:, :], preferred_element_type=_F32)
    x1 = x_ref[...] + mix
    x1_ref[...] = x1
    h2_ref[...] = _rms(x1, g2_ref[...]).T.astype(h2_ref.dtype)


def _out_proj(x, attn, conv, attn_g, conv_g, w_bf16, norm2_g, tm=512):
    t, d = x.shape
    row = lambda w: pl.BlockSpec((tm, w), lambda i: (i, 0))
    vec = lambda w: pl.BlockSpec((1, w), lambda i: (0, 0))
    return pl.pallas_call(
        _out_proj_kernel,
        grid=(t // tm,),
        in_specs=[row(d), row(ATTN_WIDTH), row(CONV_WIDTH), vec(ATTN_WIDTH), vec(CONV_WIDTH),
                  pl.BlockSpec((d, d), lambda i: (0, 0)), vec(d)],
        out_specs=[row(d), pl.BlockSpec((d, tm), lambda i: (0, i))],
        out_shape=[jax.ShapeDtypeStruct((t, d), _F32), jax.ShapeDtypeStruct((d, t), _BF16)],
        compiler_params=_params(("parallel",)),
        name="out_proj",
    )(x, attn, conv, attn_g.reshape(1, -1), conv_g.reshape(1, -1), w_bf16, norm2_g.reshape(1, -1))


def _rank_top(work, order, rounds, exact):
    (rank, values, picked), = _rank_top_many([work], [order], rounds, exact)
    return rank, values, picked


def _rank_top_many(works, orders, rounds, exact):
    big = float(2 ** 30)
    works = list(works)
    ranks = [jnp.full(w.shape, float(rounds), _F32) for w in works]
    values = [[] for _ in works]
    for a in range(rounds):
        for i, order in enumerate(orders):
            m = jnp.max(works[i], axis=0, keepdims=True)
            sel = works[i] == m
            if exact:
                first = jnp.min(jnp.where(sel, order, big), axis=0, keepdims=True)
                sel = order == first
            ranks[i] = jnp.where(sel, float(a), ranks[i])
            works[i] = jnp.where(sel, -jnp.inf, works[i])
            values[i].append(m)
    picked = [jnp.sum((r < float(rounds)).astype(_F32), axis=0, keepdims=True) for r in ranks]
    return list(zip(ranks, values, picked))


def _stack_rows(rows):
    sub = lax.broadcasted_iota(jnp.int32, (len(rows), rows[0].shape[1]), 0)
    out = jnp.broadcast_to(rows[-1], sub.shape)
    for i in range(len(rows) - 2, -1, -1):
        out = jnp.where(sub == i, rows[i], out)
    return out


ROUTE_NB = 8


def _route_tiles(tiles, exact):
    k = PEER_TOPK
    stage1 = []
    for s1, s2 in tiles:
        key_order = lax.broadcasted_iota(jnp.int32, s1.shape, 0).astype(_F32)
        (rank1, v1, picked1), (rank2, v2, picked2) = _rank_top_many([s1, s2], [key_order, key_order], k, exact)
        v2_lo, v2_hi = _stack_rows(v2[:8]), _stack_rows(v2[8:])
        sub8 = lax.broadcasted_iota(jnp.int32, v2_lo.shape, 0).astype(_F32)
        sums = [v1[0] + v2_lo, v1[0] + v2_hi] + [v1[a] + v2_lo for a in range(1, k)]
        order = [sub8, sub8 + 8.0] + [sub8 + float(k * a) for a in range(1, k)]
        stage1.append((rank1, rank2, v1[0], v2[0], (picked1 == float(k)) & (picked2 == float(k)),
                       jnp.concatenate(sums, axis=0), jnp.concatenate(order, axis=0)))
    stage2 = _rank_top_many([t[5] for t in stage1], [t[6] for t in stage1], k, exact)
    out = []
    for (s1, s2), (rank1, rank2, top1, top2, clean, cand, _), (cand_rank, _, picked3) in zip(tiles, stage1, stage2):
        tie_free = jnp.all(clean & (picked3 == float(k)))
        chosen = (cand_rank < float(k)).astype(_F32)
        z = jnp.sum(chosen * jnp.exp(cand - (top1 + top2)), axis=0, keepdims=True)
        n = [jnp.sum(chosen[0:2 * 8], axis=0, keepdims=True)]
        for a in range(1, k):
            n.append(jnp.sum(chosen[(a + 1) * 8:(a + 2) * 8], axis=0, keepdims=True))
        n1 = jnp.zeros_like(s1)
        for a in range(k):
            n1 = jnp.where(rank1 == float(a), n[a], n1)
        c1 = jnp.exp(s1 - top1) / z
        p2 = jnp.exp(s2 - top2)
        out.append((rank2, p2, n1, c1, tie_free))
    return out


def _twice_bf16(x):
    bits = pltpu.bitcast(x.astype(_BF16).astype(_F32), jnp.uint32) >> 16
    return bits | (bits << 16)


def _route_kernel(h_ref, wq_ref, keys_ref, r2_ref, p2_ref, n1_ref, c1_ref, s_s, *, tm):
    qt = jnp.dot(wq_ref[...], h_ref[...], preferred_element_type=_F32)
    groups = tm // LANES
    for hc in range(2 * PEER_HEADS):
        q_hc = qt[hc * PEER_HALF_DIM:(hc + 1) * PEER_HALF_DIM, :].astype(_BF16)
        s_hc = jnp.dot(keys_ref[hc], q_hc, preferred_element_type=_F32)
        for g in range(groups):
            s_s[hc, g] = s_hc[:, g * LANES:(g + 1) * LANES]

    def body(h, carry):
        def emit(g, rank2, p2, n1, c1):
            r2_ref[h, g] = pltpu.bitcast(rank2.astype(_BF16), jnp.uint32)
            p2_ref[h, g] = pltpu.bitcast(p2.astype(_BF16), jnp.uint32)
            n1_ref[h, g] = _twice_bf16(n1)
            c1_ref[h, g] = _twice_bf16(c1)

        tiles = [(s_s[2 * h, g], s_s[2 * h + 1, g]) for g in range(groups)]
        for g, (*quick, tie_free) in enumerate(_route_tiles(tiles, exact=False)):
            emit(g, *quick)

            @pl.when(jnp.logical_not(tie_free))
            def _(g=g):
                emit(g, *_route_tiles(tiles[g:g + 1], exact=True)[0][:4])

        return carry

    lax.fori_loop(0, PEER_HEADS, body, 0)


def _route(h2_t, wq_t_bf16, keys_bf16, tm=256):
    d, t = h2_t.shape
    groups = tm // LANES
    shapes = [(PEER_N_KEYS // 2, LANES)] * 2 + [(PEER_N_KEYS, LANES)] * 2
    out = [jax.ShapeDtypeStruct((PEER_HEADS, t // LANES) + s, jnp.uint32) for s in shapes]
    ospec = [pl.BlockSpec((PEER_HEADS, groups) + s, lambda i: (0, i, 0, 0)) for s in shapes]
    kern = functools.partial(_route_kernel, tm=tm)
    return pl.pallas_call(
        kern,
        grid=(t // tm,),
        in_specs=[
            pl.BlockSpec((d, tm), lambda i: (0, i)),
            pl.BlockSpec(wq_t_bf16.shape, lambda i: (0, 0)),
            pl.BlockSpec(keys_bf16.shape, lambda i: (0, 0, 0)),
        ],
        out_specs=ospec,
        out_shape=out,
        scratch_shapes=[pltpu.VMEM((2 * PEER_HEADS, groups, PEER_N_KEYS, LANES), _F32)],
        compiler_params=_params(("parallel",)),
        name="route",
    )(h2_t, wq_t_bf16, keys_bf16)


def _gelu(x):
    return 0.5 * x * (1.0 + lax.erf(x * (1.0 / math.sqrt(2.0))))


EXP_TE = 1024
EXP_TOK = 256
EXP_SUB = 512
EXP_ROWS = 16
EXP_OUT = 512
EXP_E = 1


def _experts_kernel(h_ref, u_ref, vt_ref, r2_ref, p2_ref, n1_ref, c1_ref, x1_ref, o_ref, acc_s, a_s, *, te):
    j = pl.program_id(1)
    tm = h_ref.shape[1]

    @pl.when(j == 0)
    def _():
        acc_s[...] = jnp.zeros_like(acc_s)

    rows = EXP_SUB // PEER_N_KEYS

    def first(tc, ec):
        return jnp.dot(u_ref[ec * EXP_SUB:(ec + 1) * EXP_SUB, :], h_ref[:, tc * EXP_TOK:(tc + 1) * EXP_TOK],
                       preferred_element_type=_F32)

    def gates(tc, ec, act, piece):
        ers = range(piece * EXP_E, (piece + 1) * EXP_E)
        e1s = [j * (te // PEER_N_KEYS) + ec * rows + er for er in ers]
        for gi in range(EXP_TOK // LANES):
            g = tc * (EXP_TOK // LANES) + gi
            lanes = slice(gi * LANES, (gi + 1) * LANES)
            wide = lambda ref, h, e1: pltpu.bitcast(
                jnp.broadcast_to(ref[h, g, pl.ds(e1, 1), :], (EXP_ROWS // 2, LANES)), _BF16)
            n1 = [[wide(n1_ref, h, e1) for h in range(PEER_HEADS)] for e1 in e1s]
            c1 = [[wide(c1_ref, h, e1) for h in range(PEER_HEADS)] for e1 in e1s]
            for sb in range(PEER_N_KEYS // EXP_ROWS):
                words = slice(sb * EXP_ROWS // 2, (sb + 1) * EXP_ROWS // 2)
                gate = [jnp.zeros((EXP_ROWS, LANES), _BF16) for _ in ers]
                for h in range(PEER_HEADS):
                    r2 = pltpu.bitcast(r2_ref[h, g, words, :], _BF16)
                    p2 = pltpu.bitcast(p2_ref[h, g, words, :], _BF16)
                    for k in range(len(e1s)):
                        gate[k] = gate[k] + jnp.where(r2 < n1[k][h], p2 * c1[k][h], jnp.zeros((), _BF16))
                for k, er in enumerate(ers):
                    src = er * PEER_N_KEYS + sb * EXP_ROWS
                    dst = ec * EXP_SUB + src
                    a_s[dst:dst + EXP_ROWS, tc * EXP_TOK + gi * LANES:tc * EXP_TOK + (gi + 1) * LANES] = (
                        _gelu(act[src:src + EXP_ROWS, lanes]).astype(_BF16) * gate[k])

    def second(tc, mc):
        tok = slice(tc * EXP_TOK, (tc + 1) * EXP_TOK)
        out_rows = slice(mc * EXP_OUT, (mc + 1) * EXP_OUT)
        acc_s[out_rows, tok] += jnp.dot(vt_ref[0, out_rows, :], a_s[:, tok], preferred_element_type=_F32)

    chains = [(tc, ec) for tc in range(tm // EXP_TOK) for ec in range(te // EXP_SUB)]
    per_tok = te // EXP_SUB
    n_out = acc_s.shape[0] // EXP_OUT
    acts = {0: first(*chains[0])}
    pending = []
    pieces = rows // EXP_E
    for ci, (tc, ec) in enumerate(chains):
        for piece in range(pieces):
            gates(tc, ec, acts[ci], piece)
            if piece == 0 and ci + 1 < len(chains):
                acts[ci + 1] = first(*chains[ci + 1])
            elif pending and piece % 2 == 1:
                second(*pending.pop(0))
        del acts[ci]
        if ec == per_tok - 1:
            pending += [(tc, mc) for mc in range(n_out)]
    for item in pending:
        second(*item)

    @pl.when(j == pl.num_programs(1) - 1)
    def _():
        o_ref[...] = x1_ref[...] + acc_s[...].T


def _experts(h2_t, x1, u_bf16, vt_blocks, route, tm=512):
    d, t = h2_t.shape
    nj, _, te = vt_blocks.shape
    once = pl.Buffered(1)
    rspec = [pl.BlockSpec((PEER_HEADS, tm // LANES) + r.shape[2:], lambda i, j: (0, i, 0, 0), pipeline_mode=once)
             for r in route]
    kern = functools.partial(_experts_kernel, te=te)
    return pl.pallas_call(
        kern,
        grid=(t // tm, nj),
        in_specs=[
            pl.BlockSpec((d, tm), lambda i, j: (0, i), pipeline_mode=once),
            pl.BlockSpec((te, d), lambda i, j: (j, 0)),
            pl.BlockSpec((1, d, te), lambda i, j: (j, 0, 0)),
            *rspec,
            pl.BlockSpec((tm, d), lambda i, j: (i, 0), pipeline_mode=once),
        ],
        out_specs=pl.BlockSpec((tm, d), lambda i, j: (i, 0)),
        out_shape=jax.ShapeDtypeStruct((t, d), _F32),
        scratch_shapes=[pltpu.VMEM((d, tm), _F32), pltpu.VMEM((te, tm), _BF16)],
        compiler_params=_params(("parallel", "arbitrary")),
        name="experts",
    )(h2_t, u_bf16, vt_blocks, *route, x1)


def _trunk(x, seq, layers, rope):
    b, s, d = x.shape
    x = x.reshape(b * s, d)
    for p in layers:
        u = _in_proj(x, p["norm1_g"], p["w_in"])
        qk = _qk_prep(u, p["q_norm_g"], p["k_norm_g"], rope, seq)
        attn = _attention(qk, u, seq)
        conv = _conv(u, p["conv_dw_w"], p["conv_dw_b"], p["conv_ln_g"], p["conv_ln_b"], seq)
        x1, h2 = _out_proj(x, attn, conv, p["attn_out_g"], p["conv_out_g"], p["w_out"], p["norm2_g"])
        route = _route(h2, p["peer_wq_t"], p["peer_keys"])
        x = _experts(h2, x1, p["peer_u"], p["peer_v_t"], route)
    return x.reshape(b, s, d)


def kernel(x_prompt, x_sample, norm1_g, w_in, q_norm_g, k_norm_g, conv_dw_w, conv_dw_b, conv_ln_g, conv_ln_b,
           attn_out_g, conv_out_g, w_out, norm2_g, peer_wq, peer_keys, peer_u, peer_v):
    depth = w_in.shape[0]
    layers = []
    for l in range(depth):
        layers.append(dict(
            norm1_g=norm1_g[l], w_in=w_in[l].astype(_BF16), q_norm_g=q_norm_g[l], k_norm_g=k_norm_g[l],
            conv_dw_w=conv_dw_w[l], conv_dw_b=conv_dw_b[l], conv_ln_g=conv_ln_g[l], conv_ln_b=conv_ln_b[l],
            attn_out_g=attn_out_g[l], conv_out_g=conv_out_g[l], w_out=w_out[l].astype(_BF16), norm2_g=norm2_g[l],
            peer_wq_t=peer_wq[l].T.astype(_BF16),
            peer_keys=peer_keys[l].reshape(2 * PEER_HEADS, PEER_N_KEYS, PEER_HALF_DIM).astype(_BF16),
            peer_u=peer_u[l].astype(_BF16),
            peer_v_t=peer_v[l].astype(_BF16).reshape(-1, EXP_TE, D_MODEL).transpose(0, 2, 1),
        ))
    outs = []
    for x in (x_prompt, x_sample):
        seq = x.shape[1]
        outs.append(_trunk(x, seq, layers, _rope_tables(seq)))
    return tuple(outs)
```

```python
import functools
import math

import jax
import jax.numpy as jnp
from jax import lax
from jax.experimental import pallas as pl
from jax.experimental.pallas import tpu as pltpu

D_MODEL = 2048
ATTN_HEADS = 16
HEAD_DIM = 64
ATTN_WIDTH = ATTN_HEADS * HEAD_DIM
CONV_WIDTH = D_MODEL - ATTN_WIDTH
CONV_KERNEL = 31
IN_WIDTH = 3 * ATTN_WIDTH + 2 * CONV_WIDTH
DILATIONS = (1, 4, 16)
HALF_WINDOW = 64
ROPE_THETA = 10000.0
NEG_INF = -1e30
PEER_HEADS = 8
PEER_N_KEYS = 128
PEER_N_EXPERTS = PEER_N_KEYS * PEER_N_KEYS
PEER_TOPK = 16
PEER_HALF_DIM = 128
EPS = 1e-6

LANES = 128
SUBLANES = 8
VMEM_LIMIT = 56 * 1024 * 1024

_BF16 = jnp.bfloat16
_F32 = jnp.float32


def _params(semantics):
    return pltpu.CompilerParams(dimension_semantics=semantics, vmem_limit_bytes=VMEM_LIMIT)


def _in_proj_kernel(x_ref, g_ref, w_ref, o_ref, h_ref):
    @pl.when(pl.program_id(1) == 0)
    def _():
        x = x_ref[...]
        ms = jnp.mean(x * x, axis=-1, keepdims=True)
        h_ref[...] = (x * lax.rsqrt(ms + EPS) * g_ref[...]).astype(_BF16)

    o_ref[...] = jnp.dot(h_ref[...], w_ref[...], preferred_element_type=_F32).astype(o_ref.dtype)


def _in_proj(x, g, w_bf16, tm=1024, tn=1024):
    t, d = x.shape
    n = w_bf16.shape[1]
    return pl.pallas_call(
        _in_proj_kernel,
        grid=(t // tm, n // tn),
        in_specs=[
            pl.BlockSpec((tm, d), lambda i, j: (i, 0)),
            pl.BlockSpec((1, d), lambda i, j: (0, 0)),
            pl.BlockSpec((d, tn), lambda i, j: (0, j)),
        ],
        out_specs=pl.BlockSpec((tm, tn), lambda i, j: (i, j)),
        out_shape=jax.ShapeDtypeStruct((t, n), _BF16),
        scratch_shapes=[pltpu.VMEM((tm, d), _BF16)],
        compiler_params=_params(("parallel", "arbitrary")),
        name="in_proj",
    )(x, g.reshape(1, d), w_bf16)


def _rope_tables(seq):
    half = HEAD_DIM // 2
    inv_freq = ROPE_THETA ** (-jnp.arange(0, HEAD_DIM, 2, dtype=_F32) / HEAD_DIM)
    ang = jnp.arange(seq, dtype=_F32)[:, None] * inv_freq[None, :]
    cos, sin = jnp.cos(ang), jnp.sin(ang)
    zero = jnp.zeros_like(sin)
    cos_t = jnp.tile(cos, (1, LANES // half))
    sin_lo = jnp.tile(jnp.concatenate([-sin, zero], axis=1), (1, LANES // HEAD_DIM))
    sin_hi = jnp.tile(jnp.concatenate([zero, sin], axis=1), (1, LANES // HEAD_DIM))
    return cos_t, sin_lo, sin_hi


def _qk_prep_kernel(u_ref, g_ref, cos_ref, slo_ref, shi_ref, o_ref):
    is_q = pl.program_id(1) < ATTN_WIDTH // LANES
    x = u_ref[...].astype(_F32)
    lane = lax.broadcasted_iota(jnp.int32, x.shape, 1)
    first = lane < HEAD_DIM
    x2 = x * x
    ss0 = jnp.sum(jnp.where(first, x2, 0.0), axis=-1, keepdims=True)
    ss1 = jnp.sum(jnp.where(first, 0.0, x2), axis=-1, keepdims=True)
    ms = jnp.where(first, ss0, ss1) * (1.0 / HEAD_DIM)
    g = jnp.where(is_q, g_ref[0:1, :], g_ref[1:2, :])
    y = x * lax.rsqrt(ms + EPS) * g
    half = HEAD_DIM // 2
    y = y * cos_ref[...] + pltpu.roll(y, LANES - half, 1) * slo_ref[...] + pltpu.roll(y, half, 1) * shi_ref[...]
    y = y * jnp.where(is_q, HEAD_DIM ** -0.5, 1.0)
    o_ref[...] = y.astype(o_ref.dtype)


def _qk_prep(u, q_g, k_g, rope, seq, tr=1024):
    t = u.shape[0]
    nblk = seq // tr
    g2 = jnp.stack([jnp.tile(q_g, LANES // HEAD_DIM), jnp.tile(k_g, LANES // HEAD_DIM)])
    tab = pl.BlockSpec((tr, LANES), lambda i, j: (i % nblk, 0))
    return pl.pallas_call(
        _qk_prep_kernel,
        grid=(t // tr, 2 * ATTN_WIDTH // LANES),
        in_specs=[
            pl.BlockSpec((tr, LANES), lambda i, j: (i, j)),
            pl.BlockSpec((2, LANES), lambda i, j: (0, 0)),
            tab, tab, tab,
        ],
        out_specs=pl.BlockSpec((tr, LANES), lambda i, j: (i, j)),
        out_shape=jax.ShapeDtypeStruct((t, 2 * ATTN_WIDTH), _BF16),
        compiler_params=_params(("parallel", "parallel")),
        name="qk_prep",
    )(u, g2, *rope)


ATT_QB = 128
ATT_W = ATT_QB + 2 * HALF_WINDOW
ATT_HALO = HALF_WINDOW * max(DILATIONS)
ATT_UNROLL = 4


def _attention_kernel(q_ref, k_ref, kl_ref, kr_ref, v_ref, vl_ref, vr_ref, o_ref,
                      q_s, k_s, v_s, acc_s, m_s, l_s, *, seq, chunk):
    nch = seq // chunk
    cpos = (pl.program_id(0) % nch) * chunk
    halo = ATT_HALO

    q_s[...] = q_ref[...].astype(_F32)
    k_s[0:halo, :] = kl_ref[...].astype(_F32)
    k_s[halo:halo + chunk, :] = k_ref[...].astype(_F32)
    k_s[halo + chunk:, :] = kr_ref[...].astype(_F32)
    v_s[0:halo, :] = vl_ref[...].astype(_F32)
    v_s[halo:halo + chunk, :] = v_ref[...].astype(_F32)
    v_s[halo + chunk:, :] = vr_ref[...].astype(_F32)

    lane = lax.broadcasted_iota(jnp.int32, (1, LANES), 1)
    first = lane < HEAD_DIM
    row2 = lax.broadcasted_iota(jnp.int32, (2 * ATT_QB, LANES), 0)
    lane2 = lax.broadcasted_iota(jnp.int32, (2 * ATT_QB, LANES), 1)
    own = ((row2 < ATT_QB) == (lane2 < HEAD_DIM)).astype(_F32)
    qi = lax.broadcasted_iota(jnp.int32, (2 * ATT_QB, ATT_W), 0) % ATT_QB
    kj = lax.broadcasted_iota(jnp.int32, (2 * ATT_QB, ATT_W), 1)
    nt = (((1,), (1,)), ((), ()))

    for bi, dil in enumerate(DILATIONS):
        nblk = chunk // dil // ATT_QB

        def scores(n, dil=dil, nblk=nblk):
            r = n // nblk
            l0 = (n % nblk) * ATT_QB
            q_start = l0 * dil + r
            k_start = halo + (l0 - HALF_WINDOW) * dil + r
            if dil == 1:
                q_idx = pl.ds(q_start, ATT_QB)
                k_idx = pl.ds(k_start, ATT_W)
            else:
                q_idx = pl.ds(q_start, ATT_QB, stride=dil)
                k_idx = pl.ds(k_start, ATT_W, stride=dil)
            q2 = q_s[q_idx, :]
            qs = (jnp.concatenate([q2, q2], axis=0) * own).astype(_BF16)
            s = lax.dot_general(qs, k_s[k_idx, :].astype(_BF16), nt, preferred_element_type=_F32)
            base = cpos + (l0 - HALF_WINDOW) * dil + r
            j_lo = (jnp.maximum(-base, 0) + dil - 1) // dil
            j_hi = (jnp.maximum(seq - base, 0) + dil - 1) // dil
            lo = jnp.maximum(qi, j_lo)
            hi = jnp.minimum(qi + 2 * HALF_WINDOW, j_hi - 1)
            return q_idx, k_idx, jnp.where((kj >= lo) & (kj <= hi), s, NEG_INF)

        def softmax(s):
            m = jnp.max(s, axis=-1, keepdims=True)
            p = jnp.exp(s - m)
            return m, jnp.sum(p, axis=-1, keepdims=True), p.astype(_BF16)

        def merge(q_idx, k_idx, m, l, p, bi=bi):
            o = jnp.dot(p, v_s[k_idx, :].astype(_BF16), preferred_element_type=_F32)
            o2 = jnp.where(first, o[:ATT_QB], o[ATT_QB:])
            m2 = jnp.where(first, m[:ATT_QB], m[ATT_QB:])
            l2 = jnp.where(first, l[:ATT_QB], l[ATT_QB:])
            if bi == 0:
                acc_s[q_idx, :] = o2
                m_s[q_idx, :] = m2
                l_s[q_idx, :] = l2
            else:
                m_old = m_s[q_idx, :]
                m_new = jnp.maximum(m_old, m2)
                a_old = jnp.exp(m_old - m_new)
                a_new = jnp.exp(m2 - m_new)
                acc_s[q_idx, :] = acc_s[q_idx, :] * a_old + o2 * a_new
                l_s[q_idx, :] = l_s[q_idx, :] * a_old + l2 * a_new
                m_s[q_idx, :] = m_new

        def body(it, carry):
            blocks = [scores(ATT_UNROLL * it + b) for b in range(ATT_UNROLL)]
            soft = [softmax(s) for _, _, s in blocks]
            for (q_idx, k_idx, _), (m, l, p) in zip(blocks, soft):
                merge(q_idx, k_idx, m, l, p)
            return carry

        lax.fori_loop(0, chunk // ATT_QB // ATT_UNROLL, body, 0)

    o_ref[...] = (acc_s[...] / l_s[...]).astype(o_ref.dtype)


def _attention(qk, u, seq, chunk=4096):
    t = u.shape[0]
    chunk = min(chunk, seq)
    halo = ATT_HALO
    per = chunk // halo
    last = t // halo - 1
    n_hp = ATTN_WIDTH // LANES
    main = lambda off: pl.BlockSpec((chunk, LANES), lambda g, h: (g, off + h))
    left = lambda off: pl.BlockSpec((halo, LANES), lambda g, h: (jnp.maximum(g * per - 1, 0), off + h))
    right = lambda off: pl.BlockSpec((halo, LANES), lambda g, h: (jnp.minimum((g + 1) * per, last), off + h))
    v_off = 2 * ATTN_WIDTH // LANES
    kern = functools.partial(_attention_kernel, seq=seq, chunk=chunk)
    return pl.pallas_call(
        kern,
        grid=(t // chunk, n_hp),
        in_specs=[main(0), main(n_hp), left(n_hp), right(n_hp), main(v_off), left(v_off), right(v_off)],
        out_specs=pl.BlockSpec((chunk, LANES), lambda g, h: (g, h)),
        out_shape=jax.ShapeDtypeStruct((t, ATTN_WIDTH), _BF16),
        scratch_shapes=[
            pltpu.VMEM((chunk, LANES), _F32),
            pltpu.VMEM((chunk + 2 * halo, LANES), _F32),
            pltpu.VMEM((chunk + 2 * halo, LANES), _F32),
            pltpu.VMEM((chunk, LANES), _F32),
            pltpu.VMEM((chunk, LANES), _F32),
            pltpu.VMEM((chunk, LANES), _F32),
        ],
        compiler_params=_params(("parallel", "parallel")),
        name="attention",
    )(qk, qk, qk, qk, u, u, u)


CONV_PAD = CONV_KERNEL // 2
CONV_HALO = 16
CONV_RB = 64
CONV_CB = 256


def _conv_kernel(a_ref, al_ref, ar_ref, g_ref, gl_ref, gr_ref, w_ref, b_ref, lng_ref, lnb_ref, o_ref,
                 c_s, y_s, *, seq, tr):
    nblk = seq // tr
    c = pl.program_id(0) % nblk

    def glu(a, g):
        return a.astype(_F32) * jax.nn.sigmoid(g.astype(_F32))

    c_s[0:CONV_HALO, :] = glu(al_ref[...], gl_ref[...]) * (c > 0).astype(_F32)
    c_s[CONV_HALO:CONV_HALO + tr, :] = glu(a_ref[...], g_ref[...])
    c_s[CONV_HALO + tr:, :] = glu(ar_ref[...], gr_ref[...]) * (c < nblk - 1).astype(_F32)

    shift = CONV_HALO - CONV_PAD
    for cb in range(CONV_WIDTH // CONV_CB):
        cols = slice(cb * CONV_CB, (cb + 1) * CONV_CB)
        w = w_ref[:, cols]
        bias = b_ref[:, cols]

        def body(rb, carry, cols=cols, w=w, bias=bias):
            i0 = pl.multiple_of(rb * CONV_RB, CONV_RB)
            win = c_s[pl.ds(i0, CONV_RB + 2 * CONV_HALO), cols]
            acc = jnp.zeros((CONV_RB, CONV_CB), _F32) + bias
            for phase in range(SUBLANES):
                shifted = pltpu.roll(win, win.shape[0] - phase, 0) if phase else win
                for off in range(phase, CONV_KERNEL + shift, SUBLANES):
                    tap = off - shift
                    if 0 <= tap < CONV_KERNEL:
                        acc = acc + shifted[off - phase:off - phase + CONV_RB, :] * w[tap:tap + 1, :]
            y_s[pl.ds(i0, CONV_RB), cols] = acc
            return carry

        lax.fori_loop(0, tr // CONV_RB, body, 0)

    y = y_s[...]
    mu = jnp.mean(y, axis=-1, keepdims=True)
    yc = y - mu
    var = jnp.mean(yc * yc, axis=-1, keepdims=True)
    z = yc * lax.rsqrt(var + EPS) * lng_ref[...] + lnb_ref[...]
    o_ref[...] = (z * jax.nn.sigmoid(z)).astype(o_ref.dtype)


def _conv(u, w, b, ln_g, ln_b, seq, tr=512):
    t = u.shape[0]
    per = tr // CONV_HALO
    last = t // CONV_HALO - 1
    a_blk = 3 * ATTN_WIDTH // CONV_WIDTH
    g_blk = a_blk + 1
    main = lambda col: pl.BlockSpec((tr, CONV_WIDTH), lambda i: (i, col))
    left = lambda col: pl.BlockSpec((CONV_HALO, CONV_WIDTH), lambda i: (jnp.maximum(i * per - 1, 0), col))
    right = lambda col: pl.BlockSpec((CONV_HALO, CONV_WIDTH), lambda i: (jnp.minimum((i + 1) * per, last), col))
    vec = lambda rows: pl.BlockSpec((rows, CONV_WIDTH), lambda i: (0, 0))
    kern = functools.partial(_conv_kernel, seq=seq, tr=tr)
    return pl.pallas_call(
        kern,
        grid=(t // tr,),
        in_specs=[main(a_blk), left(a_blk), right(a_blk), main(g_blk), left(g_blk), right(g_blk),
                  vec(CONV_KERNEL), vec(1), vec(1), vec(1)],
        out_specs=pl.BlockSpec((tr, CONV_WIDTH), lambda i: (i, 0)),
        out_shape=jax.ShapeDtypeStruct((t, CONV_WIDTH), _BF16),
        scratch_shapes=[pltpu.VMEM((tr + 2 * CONV_HALO, CONV_WIDTH), _F32), pltpu.VMEM((tr, CONV_WIDTH), _F32)],
        compiler_params=_params(("parallel",)),
        name="conv",
    )(u, u, u, u, u, u, w, b.reshape(1, -1), ln_g.reshape(1, -1), ln_b.reshape(1, -1))


def _rms(x, g):
    return x * lax.rsqrt(jnp.mean(x * x, axis=-1, keepdims=True) + EPS) * g


def _out_proj_kernel(x_ref, at_ref, cv_ref, ga_ref, gc_ref, w_ref, g2_ref, x1_ref, h2_ref):
    a = _rms(at_ref[...].astype(_F32), ga_ref[...]).astype(_BF16)
    c = _rms(cv_ref[...].astype(_F32), gc_ref[...]).astype(_BF16)
    mix = jnp.dot(a, w_ref[0:ATTN_WIDTH, :], preferred_element_type=_F32)
    mix = mix + jnp.dot(c, w_ref[ATTN_WIDTH:, :], preferred_element_type=_F32)
    x1 = x_ref[...] + mix
    x1_ref[...] = x1
    h2_ref[...] = _rms(x1, g2_ref[...]).T.astype(h2_ref.dtype)


def _out_proj(x, attn, conv, attn_g, conv_g, w_bf16, norm2_g, tm=512):
    t, d = x.shape
    row = lambda w: pl.BlockSpec((tm, w), lambda i: (i, 0))
    vec = lambda w: pl.BlockSpec((1, w), lambda i: (0, 0))
    return pl.pallas_call(
        _out_proj_kernel,
        grid=(t // tm,),
        in_specs=[row(d), row(ATTN_WIDTH), row(CONV_WIDTH), vec(ATTN_WIDTH), vec(CONV_WIDTH),
                  pl.BlockSpec((d, d), lambda i: (0, 0)), vec(d)],
        out_specs=[row(d), pl.BlockSpec((d, tm), lambda i: (0, i))],
        out_shape=[jax.ShapeDtypeStruct((t, d), _F32), jax.ShapeDtypeStruct((d, t), _BF16)],
        compiler_params=_params(("parallel",)),
        name="out_proj",
    )(x, attn, conv, attn_g.reshape(1, -1), conv_g.reshape(1, -1), w_bf16, norm2_g.reshape(1, -1))


def _rank_top(work, order, rounds, exact):
    (rank, values, picked), = _rank_top_many([work], [order], rounds, exact)
    return rank, values, picked


def _rank_top_many(works, orders, rounds, exact):
    big = float(2 ** 30)
    works = list(works)
    ranks = [jnp.full(w.shape, float(rounds), _F32) for w in works]
    values = [[] for _ in works]
    for a in range(rounds):
        for i, order in enumerate(orders):
            m = jnp.max(works[i], axis=0, keepdims=True)
            sel = works[i] == m
            if exact:
                first = jnp.min(jnp.where(sel, order, big), axis=0, keepdims=True)
                sel = order == first
            ranks[i] = jnp.where(sel, float(a), ranks[i])
            works[i] = jnp.where(sel, -jnp.inf, works[i])
            values[i].append(m)
    picked = [jnp.sum((r < float(rounds)).astype(_F32), axis=0, keepdims=True) for r in ranks]
    return list(zip(ranks, values, picked))


def _stack_rows(rows):
    sub = lax.broadcasted_iota(jnp.int32, (len(rows), rows[0].shape[1]), 0)
    out = jnp.broadcast_to(rows[-1], sub.shape)
    for i in range(len(rows) - 2, -1, -1):
        out = jnp.where(sub == i, rows[i], out)
    return out


ROUTE_NB = 8


def _route_tiles(tiles, exact):
    k = PEER_TOPK
    stage1 = []
    for s1, s2 in tiles:
        key_order = lax.broadcasted_iota(jnp.int32, s1.shape, 0).astype(_F32)
        (rank1, v1, picked1), (rank2, v2, picked2) = _rank_top_many([s1, s2], [key_order, key_order], k, exact)
        v2_lo, v2_hi = _stack_rows(v2[:8]), _stack_rows(v2[8:])
        sub8 = lax.broadcasted_iota(jnp.int32, v2_lo.shape, 0).astype(_F32)
        sums = [v1[0] + v2_lo, v1[0] + v2_hi] + [v1[a] + v2_lo for a in range(1, k)]
        order = [sub8, sub8 + 8.0] + [sub8 + float(k * a) for a in range(1, k)]
        stage1.append((rank1, rank2, v1[0], v2[0], (picked1 == float(k)) & (picked2 == float(k)),
                       jnp.concatenate(sums, axis=0), jnp.concatenate(order, axis=0)))
    stage2 = _rank_top_many([t[5] for t in stage1], [t[6] for t in stage1], k, exact)
    out = []
    for (s1, s2), (rank1, rank2, top1, top2, clean, cand, _), (cand_rank, _, picked3) in zip(tiles, stage1, stage2):
        tie_free = jnp.all(clean & (picked3 == float(k)))
        chosen = (cand_rank < float(k)).astype(_F32)
        z = jnp.sum(chosen * jnp.exp(cand - (top1 + top2)), axis=0, keepdims=True)
        n = [jnp.sum(chosen[0:2 * 8], axis=0, keepdims=True)]
        for a in range(1, k):
            n.append(jnp.sum(chosen[(a + 1) * 8:(a + 2) * 8], axis=0, keepdims=True))
        n1 = jnp.zeros_like(s1)
        for a in range(k):
            n1 = jnp.where(rank1 == float(a), n[a], n1)
        c1 = jnp.exp(s1 - top1) / z
        p2 = jnp.exp(s2 - top2)
        out.append((rank2, p2, n1, c1, tie_free))
    return out


def _twice_bf16(x):
    bits = pltpu.bitcast(x.astype(_BF16).astype(_F32), jnp.uint32) >> 16
    return bits | (bits << 16)


def _route_kernel(h_ref, wq_ref, keys_ref, r2_ref, p2_ref, n1_ref, c1_ref, s_s, *, tm):
    qt = jnp.dot(wq_ref[...], h_ref[...], preferred_element_type=_F32)
    groups = tm // LANES
    for hc in range(2 * PEER_HEADS):
        q_hc = qt[hc * PEER_HALF_DIM:(hc + 1) * PEER_HALF_DIM, :].astype(_BF16)
        s_hc = jnp.dot(keys_ref[hc], q_hc, preferred_element_type=_F32)
        for g in range(groups):
            s_s[hc, g] = s_hc[:, g * LANES:(g + 1) * LANES]

    def body(h, carry):
        def emit(g, rank2, p2, n1, c1):
            r2_ref[h, g] = pltpu.bitcast(rank2.astype(_BF16), jnp.uint32)
            p2_ref[h, g] = pltpu.bitcast(p2.astype(_BF16), jnp.uint32)
            n1_ref[h, g] = _twice_bf16(n1)
            c1_ref[h, g] = _twice_bf16(c1)

        tiles = [(s_s[2 * h, g], s_s[2 * h + 1, g]) for g in range(groups)]
        for g, (*quick, tie_free) in enumerate(_route_tiles(tiles, exact=False)):
            emit(g, *quick)

            @pl.when(jnp.logical_not(tie_free))
            def _(g=g):
                emit(g, *_route_tiles(tiles[g:g + 1], exact=True)[0][:4])

        return carry

    lax.fori_loop(0, PEER_HEADS, body, 0)


def _route(h2_t, wq_t_bf16, keys_bf16, tm=512):
    d, t = h2_t.shape
    groups = tm // LANES
    shapes = [(PEER_N_KEYS // 2, LANES)] * 2 + [(PEER_N_KEYS, LANES)] * 2
    out = [jax.ShapeDtypeStruct((PEER_HEADS, t // LANES) + s, jnp.uint32) for s in shapes]
    ospec = [pl.BlockSpec((PEER_HEADS, groups) + s, lambda i: (0, i, 0, 0)) for s in shapes]
    kern = functools.partial(_route_kernel, tm=tm)
    return pl.pallas_call(
        kern,
        grid=(t // tm,),
        in_specs=[
            pl.BlockSpec((d, tm), lambda i: (0, i)),
            pl.BlockSpec(wq_t_bf16.shape, lambda i: (0, 0)),
            pl.BlockSpec(keys_bf16.shape, lambda i: (0, 0, 0)),
        ],
        out_specs=ospec,
        out_shape=out,
        scratch_shapes=[pltpu.VMEM((2 * PEER_HEADS, groups, PEER_N_KEYS, LANES), _F32)],
        compiler_params=_params(("parallel",)),
        name="route",
    )(h2_t, wq_t_bf16, keys_bf16)


def _gelu(x):
    return 0.5 * x * (1.0 + lax.erf(x * (1.0 / math.sqrt(2.0))))


EXP_TE = 1024
EXP_TOK = 256
EXP_SUB = 512
EXP_ROWS = 16
EXP_OUT = 512
EXP_E = 1


def _experts_kernel(h_ref, u_ref, vt_ref, r2_ref, p2_ref, n1_ref, c1_ref, x1_ref, o_ref, acc_s, a_s, *, te):
    j = pl.program_id(1)
    tm = h_ref.shape[1]

    @pl.when(j == 0)
    def _():
        acc_s[...] = jnp.zeros_like(acc_s)

    rows = EXP_SUB // PEER_N_KEYS

    def first(tc, ec):
        return jnp.dot(u_ref[ec * EXP_SUB:(ec + 1) * EXP_SUB, :], h_ref[:, tc * EXP_TOK:(tc + 1) * EXP_TOK],
                       preferred_element_type=_F32)

    def gates(tc, ec, act, piece):
        ers = range(piece * EXP_E, (piece + 1) * EXP_E)
        e1s = [j * (te // PEER_N_KEYS) + ec * rows + er for er in ers]
        for gi in range(EXP_TOK // LANES):
            g = tc * (EXP_TOK // LANES) + gi
            lanes = slice(gi * LANES, (gi + 1) * LANES)
            wide = lambda ref, h, e1: pltpu.bitcast(
                jnp.broadcast_to(ref[h, g, pl.ds(e1, 1), :], (EXP_ROWS // 2, LANES)), _BF16)
            n1 = [[wide(n1_ref, h, e1) for h in range(PEER_HEADS)] for e1 in e1s]
            c1 = [[wide(c1_ref, h, e1) for h in range(PEER_HEADS)] for e1 in e1s]
            for sb in range(PEER_N_KEYS // EXP_ROWS):
                words = slice(sb * EXP_ROWS // 2, (sb + 1) * EXP_ROWS // 2)
                gate = [jnp.zeros((EXP_ROWS, LANES), _BF16) for _ in ers]
                for h in range(PEER_HEADS):
                    r2 = pltpu.bitcast(r2_ref[h, g, words, :], _BF16)
                    p2 = pltpu.bitcast(p2_ref[h, g, words, :], _BF16)
                    for k in range(len(e1s)):
                        gate[k] = gate[k] + jnp.where(r2 < n1[k][h], p2 * c1[k][h], jnp.zeros((), _BF16))
                for k, er in enumerate(ers):
                    src = er * PEER_N_KEYS + sb * EXP_ROWS
                    dst = ec * EXP_SUB + src
                    a_s[dst:dst + EXP_ROWS, tc * EXP_TOK + gi * LANES:tc * EXP_TOK + (gi + 1) * LANES] = (
                        _gelu(act[src:src + EXP_ROWS, lanes]).astype(_BF16) * gate[k])

    def second(tc, mc):
        tok = slice(tc * EXP_TOK, (tc + 1) * EXP_TOK)
        out_rows = slice(mc * EXP_OUT, (mc + 1) * EXP_OUT)
        acc_s[out_rows, tok] += jnp.dot(vt_ref[0, out_rows, :], a_s[:, tok], preferred_element_type=_F32)

    chains = [(tc, ec) for tc in range(tm // EXP_TOK) for ec in range(te // EXP_SUB)]
    per_tok = te // EXP_SUB
    n_out = acc_s.shape[0] // EXP_OUT
    acts = {0: first(*chains[0])}
    pending = []
    pieces = rows // EXP_E
    for ci, (tc, ec) in enumerate(chains):
        for piece in range(pieces):
            gates(tc, ec, acts[ci], piece)
            if piece == 0 and ci + 1 < len(chains):
                acts[ci + 1] = first(*chains[ci + 1])
            elif pending and piece % 2 == 1:
                second(*pending.pop(0))
        del acts[ci]
        if ec == per_tok - 1:
            pending += [(tc, mc) for mc in range(n_out)]
    for item in pending:
        second(*item)

    @pl.when(j == pl.num_programs(1) - 1)
    def _():
        o_ref[...] = x1_ref[...] + acc_s[...].T


def _experts(h2_t, x1, u_bf16, vt_blocks, route, tm=512):
    d, t = h2_t.shape
    nj, _, te = vt_blocks.shape
    once = pl.Buffered(1)
    rspec = [pl.BlockSpec((PEER_HEADS, tm // LANES) + r.shape[2:], lambda i, j: (0, i, 0, 0), pipeline_mode=once)
             for r in route]
    kern = functools.partial(_experts_kernel, te=te)
    return pl.pallas_call(
        kern,
        grid=(t // tm, nj),
        in_specs=[
            pl.BlockSpec((d, tm), lambda i, j: (0, i), pipeline_mode=once),
            pl.BlockSpec((te, d), lambda i, j: (j, 0)),
            pl.BlockSpec((1, d, te), lambda i, j: (j, 0, 0)),
            *rspec,
            pl.BlockSpec((tm, d), lambda i, j: (i, 0), pipeline_mode=once),
        ],
        out_specs=pl.BlockSpec((tm, d), lambda i, j: (i, 0)),
        out_shape=jax.ShapeDtypeStruct((t, d), _F32),
        scratch_shapes=[pltpu.VMEM((d, tm), _F32), pltpu.VMEM((te, tm), _BF16)],
        compiler_params=_params(("parallel", "arbitrary")),
        name="experts",
    )(h2_t, u_bf16, vt_blocks, *route, x1)


def _trunk(x, seq, layers, rope):
    b, s, d = x.shape
    x = x.reshape(b * s, d)
    for p in layers:
        u = _in_proj(x, p["norm1_g"], p["w_in"])
        qk = _qk_prep(u, p["q_norm_g"], p["k_norm_g"], rope, seq)
        attn = _attention(qk, u, seq)
        conv = _conv(u, p["conv_dw_w"], p["conv_dw_b"], p["conv_ln_g"], p["conv_ln_b"], seq)
        x1, h2 = _out_proj(x, attn, conv, p["attn_out_g"], p["conv_out_g"], p["w_out"], p["norm2_g"])
        route = _route(h2, p["peer_wq_t"], p["peer_keys"])
        x = _experts(h2, x1, p["peer_u"], p["peer_v_t"], route)
    return x.reshape(b, s, d)


def kernel(x_prompt, x_sample, norm1_g, w_in, q_norm_g, k_norm_g, conv_dw_w, conv_dw_b, conv_ln_g, conv_ln_b,
           attn_out_g, conv_out_g, w_out, norm2_g, peer_wq, peer_keys, peer_u, peer_v):
    depth = w_in.shape[0]
    layers = []
    for l in range(depth):
        layers.append(dict(
            norm1_g=norm1_g[l], w_in=w_in[l].astype(_BF16), q_norm_g=q_norm_g[l], k_norm_g=k_norm_g[l],
            conv_dw_w=conv_dw_w[l], conv_dw_b=conv_dw_b[l], conv_ln_g=conv_ln_g[l], conv_ln_b=conv_ln_b[l],
            attn_out_g=attn_out_g[l], conv_out_g=conv_out_g[l], w_out=w_out[l].astype(_BF16), norm2_g=norm2_g[l],
            peer_wq_t=peer_wq[l].T.astype(_BF16),
            peer_keys=peer_keys[l].reshape(2 * PEER_HEADS, PEER_N_KEYS, PEER_HALF_DIM).astype(_BF16),
            peer_u=peer_u[l].astype(_BF16),
            peer_v_t=peer_v[l].astype(_BF16).reshape(-1, EXP_TE, D_MODEL).transpose(0, 2, 1),
        ))
    outs = []
    for x in (x_prompt, x_sample):
        seq = x.shape[1]
        outs.append(_trunk(x, seq, layers, _rope_tables(seq)))
    return tuple(outs)
```
